```python
import jax, jax.numpy as jnp
from jax import lax
import numpy as np

D_MODEL = 1024
BATCH = 32
SEQ = 2048
DEPTH = 2

GRID_W = 64
CTX_LEN = 256
HEAD_DIM = 64
N_HEADS_TOTAL = D_MODEL // HEAD_DIM
A_GROUPS = N_HEADS_TOTAL // 4
NA_HEADS = N_HEADS_TOTAL // 4
SW_Q_HEADS = N_HEADS_TOTAL - A_GROUPS - NA_HEADS
SW_KV_HEADS = max(1, SW_Q_HEADS // 4)
SW_GROUP = SW_Q_HEADS // SW_KV_HEADS
A_WIDTH = A_GROUPS * HEAD_DIM
NA_WIDTH = NA_HEADS * HEAD_DIM
SW_Q_WIDTH = SW_Q_HEADS * HEAD_DIM
SW_KV_WIDTH = SW_KV_HEADS * HEAD_DIM
MIX_WIDTH = A_WIDTH + NA_WIDTH + SW_Q_WIDTH
PROJ_WIDTHS = (A_WIDTH, A_WIDTH, NA_WIDTH, NA_WIDTH, NA_WIDTH, SW_Q_WIDTH, SW_KV_WIDTH, SW_KV_WIDTH)
IN_WIDTH = sum(PROJ_WIDTHS)
CHUNK = 128
NA_ROWS = 8
NA_COLS = 16
SW_WINDOW = 128
SW_BLOCK = 128
ROPE_BASE = 10000.0
ROPE_HALF = HEAD_DIM // 2
ROPE_FREQS = ROPE_HALF // 2
D_FF = 4 * D_MODEL
DEEPNORM_ALPHA = (2 * DEPTH) ** 0.25
DEEPNORM_BETA = (8 * DEPTH) ** -0.25
LN_EPS = 1e-5
NEG_INF = -1e30

kernel_name = 'hybrid_parallel_heads_diffusion_trunk'


def layer_norm(x, g, b):
    xf = x.astype(jnp.float32)
    mu = jnp.mean(xf, axis=-1, keepdims=True)
    var = jnp.mean(jnp.square(xf - mu), axis=-1, keepdims=True)
    return ((xf - mu) * lax.rsqrt(var + LN_EPS)).astype(x.dtype) * g + b


def modulate(x, shift, scale):
    return x * (1.0 + scale) + shift


def post_norm_residual(x, y, g, b):
    return layer_norm(DEEPNORM_ALPHA * x + y, g, b)


def split_projection(p):
    idx = [int(i) for i in np.cumsum(PROJ_WIDTHS)[:-1]]
    return jnp.split(p, idx, axis=-1)


def to_heads(t, n):
    b, l, _ = t.shape
    return t.reshape(b, l, n, HEAD_DIM).transpose(0, 2, 1, 3)


def to_gqa_q(t):
    b, l, _ = t.shape
    return t.reshape(b, l, SW_KV_HEADS, SW_GROUP, HEAD_DIM).transpose(0, 2, 3, 1, 4)


def merge_heads(t):
    b, h, l, d = t.shape
    return t.transpose(0, 2, 1, 3).reshape(b, l, h * d)


def merge_gqa(t):
    b, kv, g, l, d = t.shape
    return t.transpose(0, 3, 1, 2, 4).reshape(b, l, kv * g * d)


def _rotate(t, ang):
    t1, t2 = jnp.split(t, 2, axis=-1)
    cos, sin = jnp.cos(ang), jnp.sin(ang)
    return jnp.concatenate([t1 * cos - t2 * sin, t1 * sin + t2 * cos], axis=-1)


def axial_rope(t, row_pos, col_pos):
    inv_freq = ROPE_BASE ** (-jnp.arange(ROPE_FREQS, dtype=jnp.float32) / ROPE_FREQS)
    ang_r = row_pos.astype(jnp.float32)[:, None] * inv_freq
    ang_c = col_pos.astype(jnp.float32)[:, None] * inv_freq
    tf = t.astype(jnp.float32)
    out = jnp.concatenate([_rotate(tf[..., :ROPE_HALF], ang_r), _rotate(tf[..., ROPE_HALF:], ang_c)], axis=-1)
    return out.astype(t.dtype)


def attn_probs(scores, sink=None):
    parts = [s.astype(jnp.float32) for s in scores]
    if sink is not None:
        parts.append(jnp.broadcast_to(sink.astype(jnp.float32), parts[0].shape[:-1] + (1,)))
    p = jax.nn.softmax(jnp.concatenate(parts, axis=-1), axis=-1)
    sizes = [s.shape[-1] for s in scores]
    idx = [int(i) for i in np.cumsum(sizes)[:-1]]
    return jnp.split(p[..., :sum(sizes)], idx, axis=-1)


def gmlp_chunk_mix(u, v, ln_g, ln_b, w_s, b_s):
    b, l, _ = v.shape
    v = layer_norm(v, ln_g, ln_b).reshape(b, l // CHUNK, CHUNK, A_GROUPS, HEAD_DIM)
    mixed = jnp.einsum('gij,bnjgc->bnigc', w_s, v) + b_s.T[None, None, :, :, None]
    return u * mixed.reshape(b, l, A_WIDTH)


def ctx_self_attention(q, k, v, sink):
    s = jnp.einsum('bkgqd,bkcd->bkgqc', q, k) * (HEAD_DIM ** -0.5)
    (p,) = attn_probs([s], sink)
    return jnp.einsum('bkgqc,bkcd->bkgqd', p.astype(v.dtype), v)


def neighbourhood_attention(q, k, v, kc, vc, rpb):
    b, h, s, d = q.shape
    rows = s // GRID_W
    win_r = min(NA_ROWS, rows)
    qg = q.reshape(b, h, rows, GRID_W, d)
    kg = k.reshape(b, h, rows, GRID_W, d)
    vg = v.reshape(b, h, rows, GRID_W, d)
    cq = jnp.arange(GRID_W)
    cs = jnp.clip(cq - NA_COLS // 2, 0, GRID_W - NA_COLS)
    col_ok = (cq[None, :] >= cs[:, None]) & (cq[None, :] < cs[:, None] + NA_COLS)
    dc = jnp.clip(cq[None, :] - cq[:, None], -(NA_COLS - 1), NA_COLS - 1) + NA_COLS - 1
    rpb_col = rpb[:, :, dc]
    mask = jnp.tile(col_ok, (1, win_r))
    scale = HEAD_DIM ** -0.5

    def one_row(args):
        q_r, r = args
        rs = jnp.clip(r - win_r // 2, 0, rows - win_r)
        k_r = lax.dynamic_slice_in_dim(kg, rs, win_r, axis=2).reshape(b, h, win_r * GRID_W, d)
        v_r = lax.dynamic_slice_in_dim(vg, rs, win_r, axis=2).reshape(b, h, win_r * GRID_W, d)
        dr = rs + jnp.arange(win_r) - r + NA_ROWS - 1
        bias = jnp.take(rpb_col, dr, axis=1).transpose(0, 2, 1, 3).reshape(h, GRID_W, win_r * GRID_W)
        s_loc = jnp.einsum('bhqd,bhkd->bhqk', q_r, k_r).astype(jnp.float32) * scale + bias.astype(jnp.float32)
        s_loc = jnp.where(mask, s_loc, NEG_INF)
        s_ctx = jnp.einsum('bhqd,bhcd->bhqc', q_r, kc) * scale
        p_loc, p_ctx = attn_probs([s_loc, s_ctx])
        return (jnp.einsum('bhqk,bhkd->bhqd', p_loc.astype(v.dtype), v_r)
                + jnp.einsum('bhqc,bhcd->bhqd', p_ctx.astype(v.dtype), vc))

    out = lax.map(one_row, (jnp.moveaxis(qg, 2, 0), jnp.arange(rows)))
    return jnp.moveaxis(out, 0, 2).reshape(b, h, s, d)


def sliding_window_attention(q, k, v, kc, vc, sink):
    b, kv, g, s, d = q.shape
    nb = s // SW_BLOCK
    pad = ((0, 0), (0, 0), (SW_BLOCK, SW_BLOCK), (0, 0))
    kp = jnp.pad(k, pad).reshape(b, kv, nb + 2, SW_BLOCK, d)
    vp = jnp.pad(v, pad).reshape(b, kv, nb + 2, SW_BLOCK, d)
    band = lambda t: jnp.concatenate([t[:, :, :-2], t[:, :, 1:-1], t[:, :, 2:]], axis=3)
    kb, vb = band(kp), band(vp)
    qb = q.reshape(b, kv, g, nb, SW_BLOCK, d)
    k_rel = jnp.arange(3 * SW_BLOCK) - SW_BLOCK
    band_ok = jnp.abs(k_rel[None, :] - jnp.arange(SW_BLOCK)[:, None]) <= SW_WINDOW
    scale = HEAD_DIM ** -0.5

    def one_block(args):
        q_i, k_i, v_i, i = args
        k_abs = i * SW_BLOCK + k_rel
        ok = band_ok & ((k_abs >= 0) & (k_abs < s))[None, :]
        s_loc = jnp.einsum('bkgqd,bknd->bkgqn', q_i, k_i).astype(jnp.float32) * scale
        s_loc = jnp.where(ok, s_loc, NEG_INF)
        s_ctx = jnp.einsum('bkgqd,bkcd->bkgqc', q_i, kc) * scale
        p_loc, p_ctx = attn_probs([s_loc, s_ctx], sink)
        return (jnp.einsum('bkgqn,bknd->bkgqd', p_loc.astype(v.dtype), v_i)
                + jnp.einsum('bkgqc,bkcd->bkgqd', p_ctx.astype(v.dtype), vc))

    out = lax.map(one_block, (jnp.moveaxis(qb, 3, 0), jnp.moveaxis(kb, 2, 0), jnp.moveaxis(vb, 2, 0), jnp.arange(nb)))
    return jnp.moveaxis(out, 0, 3).reshape(b, kv, g, s, d)


def squared_relu_mlp(h, w1, w2):
    return jnp.square(jax.nn.relu(h @ w1)) @ w2


def trunk_layer(x, xc, c, c_ctx, w_mod, b_mod, w_in, a_ln_g, a_ln_b, a_ws, a_bs, na_rpb, sw_sink,
                w_out, ln1_g, ln1_b, w1, w2, ln2_g, ln2_b, update_ctx):
    s = x.shape[1]
    pos = jnp.arange(s)
    row_pos, col_pos = pos // GRID_W, pos % GRID_W
    mod = jax.nn.silu(c) @ w_mod + b_mod
    mod_c = jax.nn.silu(c_ctx)[None] @ w_mod + b_mod
    sh1, sc1, g1, sh2, sc2, g2 = [m[:, None, :] for m in jnp.split(mod, 6, axis=-1)]
    sh1c, sc1c, g1c, sh2c, sc2c, g2c = [m[:, None, :] for m in jnp.split(mod_c, 6, axis=-1)]

    a_u, a_v, na_q, na_k, na_v, sw_q, sw_k, sw_v = split_projection(modulate(x, sh1, sc1) @ w_in)
    ca_u, ca_v, cna_q, cna_k, cna_v, csw_q, csw_k, csw_v = split_projection(modulate(xc, sh1c, sc1c) @ w_in)
    sink = sw_sink.reshape(SW_KV_HEADS, SW_GROUP)[None, :, :, None, None]

    kc_na, vc_na = to_heads(cna_k, NA_HEADS), to_heads(cna_v, NA_HEADS)
    kc_sw, vc_sw = to_heads(csw_k, SW_KV_HEADS), to_heads(csw_v, SW_KV_HEADS)

    y_a = gmlp_chunk_mix(jax.nn.gelu(a_u), jax.nn.gelu(a_v), a_ln_g, a_ln_b, a_ws, a_bs)
    y_na = neighbourhood_attention(to_heads(na_q, NA_HEADS), to_heads(na_k, NA_HEADS), to_heads(na_v, NA_HEADS),
                                   kc_na, vc_na, na_rpb)
    q_sw = axial_rope(to_gqa_q(sw_q), row_pos, col_pos)
    k_sw = axial_rope(to_heads(sw_k, SW_KV_HEADS), row_pos, col_pos)
    y_sw = sliding_window_attention(q_sw, k_sw, to_heads(sw_v, SW_KV_HEADS), kc_sw, vc_sw, sink)
    y = jnp.concatenate([y_a, merge_heads(y_na), merge_gqa(y_sw)], axis=-1) @ w_out
    x_new = post_norm_residual(x, g1 * y, ln1_g, ln1_b)
    x_new = post_norm_residual(x_new, g2 * squared_relu_mlp(modulate(x_new, sh2, sc2), w1, w2), ln2_g, ln2_b)

    if update_ctx:
        yc_a = gmlp_chunk_mix(jax.nn.gelu(ca_u), jax.nn.gelu(ca_v), a_ln_g, a_ln_b, a_ws, a_bs)
        yc_na = ctx_self_attention(to_heads(cna_q, NA_HEADS)[:, :, None], kc_na, vc_na, None)[:, :, 0]
        yc_sw = ctx_self_attention(to_gqa_q(csw_q), kc_sw, vc_sw, sink)
        yc = jnp.concatenate([yc_a, merge_heads(yc_na), merge_gqa(yc_sw)], axis=-1) @ w_out
        xc = post_norm_residual(xc, g1c * yc, ln1_g, ln1_b)
        xc = post_norm_residual(xc, g2c * squared_relu_mlp(modulate(xc, sh2c, sc2c), w1, w2), ln2_g, ln2_b)
    return x_new, xc


def setup_inputs(seed: int = 0) -> dict:
    key = jax.random.key(seed)
    ks = jax.random.split(key, 20)
    n = lambda k, shape: jax.random.normal(k, shape, jnp.float32)
    return {
        'x': n(ks[0], (BATCH, SEQ, D_MODEL)),
        'c': n(ks[1], (BATCH, D_MODEL)),
        'ctx': n(ks[2], (BATCH, CTX_LEN, D_MODEL)),
        'c_ctx': n(ks[3], (D_MODEL,)),
        'w_mod': n(ks[4], (DEPTH, D_MODEL, 6 * D_MODEL)) * D_MODEL ** -0.5,
        'b_mod': n(ks[5], (DEPTH, 6 * D_MODEL)) * 0.02,
        'w_in': n(ks[6], (DEPTH, D_MODEL, IN_WIDTH)) * D_MODEL ** -0.5,
        'a_ln_g': 1.0 + 0.02 * n(ks[7], (DEPTH, A_WIDTH)),
        'a_ln_b': 0.02 * n(ks[8], (DEPTH, A_WIDTH)),
        'a_ws': n(ks[9], (DEPTH, A_GROUPS, CHUNK, CHUNK)) * CHUNK ** -0.5,
        'a_bs': 1.0 + 0.02 * n(ks[10], (DEPTH, A_GROUPS, CHUNK)),
        'na_rpb': 0.1 * n(ks[11], (DEPTH, NA_HEADS, 2 * NA_ROWS - 1, 2 * NA_COLS - 1)),
        'sw_sink': n(ks[12], (DEPTH, SW_Q_HEADS)),
        'w_out': n(ks[13], (DEPTH, MIX_WIDTH, D_MODEL)) * (MIX_WIDTH ** -0.5 * DEEPNORM_BETA),
        'ln1_g': 1.0 + 0.02 * n(ks[14], (DEPTH, D_MODEL)),
        'ln1_b': 0.02 * n(ks[15], (DEPTH, D_MODEL)),
        'w1': n(ks[16], (DEPTH, D_MODEL, D_FF)) * D_MODEL ** -0.5,
        'w2': n(ks[17], (DEPTH, D_FF, D_MODEL)) * (D_FF ** -0.5 * DEEPNORM_BETA),
        'ln2_g': 1.0 + 0.02 * n(ks[18], (DEPTH, D_MODEL)),
        'ln2_b': 0.02 * n(ks[19], (DEPTH, D_MODEL)),
    }


def reference(x, c, ctx, c_ctx, w_mod, b_mod, w_in, a_ln_g, a_ln_b, a_ws, a_bs, na_rpb, sw_sink,
              w_out, ln1_g, ln1_b, w1, w2, ln2_g, ln2_b):
    xc = ctx
    for layer in range(DEPTH):
        x, xc = trunk_layer(x, xc, c, c_ctx, w_mod[layer], b_mod[layer], w_in[layer], a_ln_g[layer], a_ln_b[layer],
                            a_ws[layer], a_bs[layer], na_rpb[layer], sw_sink[layer], w_out[layer],
                            ln1_g[layer], ln1_b[layer], w1[layer], w2[layer], ln2_g[layer], ln2_b[layer],
                            update_ctx=layer < DEPTH - 1)
    return x
```

```python
import functools

import numpy as np
import jax
import jax.numpy as jnp
from jax import lax
from jax.experimental import pallas as pl
from jax.experimental.pallas import tpu as pltpu

F32 = jnp.float32
BF16 = jnp.bfloat16

D_MODEL = 1024
HEAD_DIM = 64
GRID_W = 64
CHUNK = 128
A_GROUPS = 4
A_WIDTH = 256
NA_HEADS = 4
NA_WIDTH = 256
NA_ROWS = 8
NA_COLS = 16
SW_KV_HEADS = 2
SW_GROUP = 4
SW_Q_WIDTH = 512
SW_KV_WIDTH = 128
SW_BLOCK = 128
SW_WINDOW = 128
IN_WIDTH = 2048
D_FF = 4096
ROPE_BASE = 10000.0
ROPE_FREQS = 16
LN_EPS = 1e-5
NEG_INF = -1e30
ATTN_SCALE = HEAD_DIM ** -0.5

LANES = 128
MOD_ROWS_PAD = 8
VMEM_LIMIT = 56 * 1024 * 1024


def _layer_norm(x, g, b):
    mu = jnp.mean(x, axis=-1, keepdims=True)
    var = jnp.mean(jnp.square(x - mu), axis=-1, keepdims=True)
    return (x - mu) * lax.rsqrt(var + LN_EPS) * g + b


def _params(n_grid):
    return pltpu.CompilerParams(dimension_semantics=("arbitrary",) * n_grid,
                                vmem_limit_bytes=VMEM_LIMIT)


def _mod_kernel(c_ref, w_ref, b_ref, o_ref):
    a = jax.nn.silu(c_ref[...]).astype(BF16)
    o_ref[0] = jnp.dot(a, w_ref[0].astype(BF16), preferred_element_type=F32) + b_ref[0]


def _modulation(cc, w_mod, b_mod, tn=1024):
    depth, d, n = w_mod.shape
    rows = cc.shape[0]
    return pl.pallas_call(
        _mod_kernel,
        grid=(depth, n // tn),
        in_specs=[pl.BlockSpec((rows, d), lambda l, j: (0, 0)),
                  pl.BlockSpec((1, d, tn), lambda l, j: (l, 0, j)),
                  pl.BlockSpec((1, 1, tn), lambda l, j: (l, 0, j))],
        out_specs=pl.BlockSpec((1, rows, tn), lambda l, j: (l, 0, j)),
        out_shape=jax.ShapeDtypeStruct((depth, rows, n), F32),
        compiler_params=_params(2),
        name="modulation",
    )(cc, w_mod, b_mod.reshape(depth, 1, n))


def _inproj_kernel(x_ref, mod_ref, w_ref, lng_ref, lnb_ref, cos_ref, sin_ref,
                   u_ref, v_ref, naq_ref, nak_ref, nav_ref, swq_ref, swk_ref, swv_ref, *, rope):
    x = x_ref[0]
    h = (x * (1.0 + mod_ref[0, 1:2, :]) + mod_ref[0, 0:1, :]).astype(BF16)

    def proj(lo, hi):
        return jnp.dot(h, w_ref[:, lo:hi], preferred_element_type=F32)

    if rope:
        cos = cos_ref[...]
        sin = sin_ref[...]
        first = (lax.broadcasted_iota(jnp.int32, (1, LANES), 1) % 32) < 16

        def rot(t):
            partner = jnp.where(first, pltpu.roll(t, LANES - 16, axis=1), pltpu.roll(t, 16, axis=1))
            return t * cos + partner * sin
    else:
        def rot(t):
            return t

    u_ref[0] = jax.nn.gelu(proj(0, 256)).astype(BF16)
    v = jax.nn.gelu(proj(256, 512))
    v_ref[0] = _layer_norm(v, lng_ref[...], lnb_ref[...]).astype(BF16)
    naq_ref[0] = (proj(512, 768) * ATTN_SCALE).astype(BF16)
    nak_ref[0] = proj(768, 1024).astype(BF16)
    nav_ref[0] = proj(1024, 1280).astype(BF16)
    q = proj(1280, 1792)
    for j in range(SW_Q_WIDTH // LANES):
        sl = slice(j * LANES, (j + 1) * LANES)
        swq_ref[0, :, sl] = (rot(q[:, sl]) * ATTN_SCALE).astype(BF16)
    kv = proj(1792, 2048)
    swk_ref[0] = rot(kv[:, :SW_KV_WIDTH]).astype(BF16)
    swv_ref[0] = kv[:, SW_KV_WIDTH:].astype(BF16)


def _in_projection(x, mod6, w_in, ln_g, ln_b, cos, sin, *, rope, tm):
    b, s, d = x.shape
    widths = (A_WIDTH, A_WIDTH, NA_WIDTH, NA_WIDTH, NA_WIDTH, SW_Q_WIDTH, SW_KV_WIDTH, SW_KV_WIDTH)
    tile = lambda w: pl.BlockSpec((1, tm, w), lambda i, j: (i, j, 0))
    const = lambda shape: pl.BlockSpec(shape, lambda i, j: (0,) * len(shape))
    return pl.pallas_call(
        functools.partial(_inproj_kernel, rope=rope),
        grid=(b, s // tm),
        in_specs=[tile(d),
                  pl.BlockSpec((1, 6, d), lambda i, j: (i, 0, 0)),
                  const((d, IN_WIDTH)),
                  const((1, A_WIDTH)), const((1, A_WIDTH)),
                  pl.BlockSpec((tm, LANES), lambda i, j: (j, 0)),
                  pl.BlockSpec((tm, LANES), lambda i, j: (j, 0))],
        out_specs=[tile(w) for w in widths],
        out_shape=[jax.ShapeDtypeStruct((b, s, w), BF16) for w in widths],
        compiler_params=_params(2),
        name="in_projection_rope" if rope else "in_projection",
    )(x, mod6, w_in, ln_g, ln_b, cos, sin)


def _gmlp_chunk(u, v, ws_all, bsm):
    m = jnp.dot(ws_all, v, preferred_element_type=F32)
    group = lax.broadcasted_iota(jnp.int32, (1, A_WIDTH), 1) // HEAD_DIM
    mixed = m[0:CHUNK]
    for g in range(1, A_GROUPS):
        mixed = jnp.where(group == g, m[g * CHUNK:(g + 1) * CHUNK], mixed)
    return u.astype(F32) * (mixed + bsm)


def _stack_na_q(q):
    head = lax.broadcasted_iota(jnp.int32, (1, NA_WIDTH), 1) // HEAD_DIM
    zero = jnp.zeros_like(q)
    return jnp.concatenate([jnp.where(head == h, q, zero) for h in range(NA_HEADS)], axis=0)


def _merge_na_o(o, n):
    head = lax.broadcasted_iota(jnp.int32, (1, NA_WIDTH), 1) // HEAD_DIM
    out = o[0:n]
    for h in range(1, NA_HEADS):
        out = jnp.where(head == h, o[h * n:(h + 1) * n], out)
    return out


def _stack_sw_q(q, n):
    low = lax.broadcasted_iota(jnp.int32, (1, LANES), 1) < HEAD_DIM
    parts = []
    for j in range(SW_GROUP):
        qj = q[:, j * LANES:(j + 1) * LANES]
        zero = jnp.zeros_like(qj)
        parts.append(jnp.where(low, qj, zero))
        parts.append(jnp.where(low, zero, qj))
    return jnp.concatenate(parts, axis=0)


def _merge_sw_o(o, n):
    low = lax.broadcasted_iota(jnp.int32, (1, LANES), 1) < HEAD_DIM
    return [jnp.where(low, o[(2 * j) * n:(2 * j + 1) * n], o[(2 * j + 1) * n:(2 * j + 2) * n])
            for j in range(SW_GROUP)]


def _qk(q, k):
    return lax.dot_general(q, k, (((1,), (1,)), ((), ())), preferred_element_type=F32)


def _softmax_pv(s, v, sink=None):
    heads, n, keys = s.shape
    m = jnp.max(s, axis=-1, keepdims=True)
    if sink is not None:
        m = jnp.maximum(m, sink)
    e = jnp.exp(s - m)
    l = jnp.sum(e, axis=-1, keepdims=True)
    if sink is not None:
        l = l + jnp.exp(sink - m)
    o = jnp.dot(e.reshape(heads * n, keys).astype(BF16), v, preferred_element_type=F32)
    return o * (1.0 / l).reshape(heads * n, 1)


def _out_proj_norm(y, x, g1, wout, ln_g, ln_b, alpha):
    yo = jnp.dot(y, wout, preferred_element_type=F32)
    return _layer_norm(alpha * x + g1 * yo, ln_g, ln_b)


def _mixer_kernel(u_ref, v_ref, naq_ref, nak_ref, nav_ref, swq_ref, swk_ref, swv_ref,
                  cnak_ref, cnav_ref, cswk_ref, cswv_ref,
                  ws_ref, bsm_ref, bias_ref, sink_ref,
                  x_ref, mod_ref, wout_ref, lng_ref, lnb_ref,
                  o_ref, y_scr, *, tq, seq, alpha):
    tile = pl.program_id(1)
    rows_per_tile = tq // GRID_W
    n_rows = seq // GRID_W
    win_keys = NA_ROWS * GRID_W

    ws_all = ws_ref[...]
    bsm = bsm_ref[...]
    for c in range(tq // CHUNK):
        sl = slice(c * CHUNK, (c + 1) * CHUNK)
        y_scr[sl, 0:A_WIDTH] = _gmlp_chunk(u_ref[0, sl, :], v_ref[0, sl, :], ws_all, bsm).astype(BF16)

    cnak = cnak_ref[0]
    cnav = cnav_ref[0]

    def na_row(i, carry):
        r = tile * rows_per_tile + i
        rs = jnp.clip(r - NA_ROWS // 2, 0, n_rows - NA_ROWS)
        off = r - rs
        kstart = pl.multiple_of(rs * GRID_W, GRID_W)
        qstart = pl.multiple_of(i * GRID_W, GRID_W)
        qs = _stack_na_q(naq_ref[0, pl.ds(qstart, GRID_W), :])
        k_loc = nak_ref[0, pl.ds(kstart, win_keys), :]
        v_loc = nav_ref[0, pl.ds(kstart, win_keys), :]
        s_loc = _qk(qs, k_loc)
        parts = []
        for jj in range(win_keys // LANES):
            parts.append(s_loc[:, jj * LANES:(jj + 1) * LANES] + bias_ref[2 * jj + NA_ROWS - 1 - off])
        parts.append(_qk(qs, cnak))
        s = jnp.concatenate(parts, axis=-1)
        o = _softmax_pv(s.reshape(1, NA_HEADS * GRID_W, s.shape[-1]),
                        jnp.concatenate([v_loc, cnav], axis=0))
        y_scr[pl.ds(qstart, GRID_W), A_WIDTH:A_WIDTH + NA_WIDTH] = _merge_na_o(o, GRID_W).astype(BF16)
        return carry

    lax.fori_loop(0, rows_per_tile, na_row, 0)

    cswk = cswk_ref[0]
    cswv = cswv_ref[0]
    sink = sink_ref[...]
    n_blocks = seq // SW_BLOCK
    qi = lax.broadcasted_iota(jnp.int32, (SW_BLOCK, SW_BLOCK), 0)
    kj = lax.broadcasted_iota(jnp.int32, (SW_BLOCK, SW_BLOCK), 1)

    def sw_block(i, carry):
        bi = tile * (tq // SW_BLOCK) + i
        t0 = bi * SW_BLOCK
        qstart = pl.multiple_of(i * SW_BLOCK, SW_BLOCK)
        p0 = pl.multiple_of(jnp.maximum(t0 - SW_BLOCK, 0), SW_BLOCK)
        c0 = pl.multiple_of(t0, SW_BLOCK)
        n0 = pl.multiple_of(jnp.minimum(t0 + SW_BLOCK, seq - SW_BLOCK), SW_BLOCK)
        qs = _stack_sw_q(swq_ref[0, pl.ds(qstart, SW_BLOCK), :], SW_BLOCK)
        kcat = jnp.concatenate([swk_ref[0, pl.ds(p0, SW_BLOCK), :], swk_ref[0, pl.ds(c0, SW_BLOCK), :],
                                swk_ref[0, pl.ds(n0, SW_BLOCK), :], cswk], axis=0)
        vcat = jnp.concatenate([swv_ref[0, pl.ds(p0, SW_BLOCK), :], swv_ref[0, pl.ds(c0, SW_BLOCK), :],
                                swv_ref[0, pl.ds(n0, SW_BLOCK), :], cswv], axis=0)
        heads = 2 * SW_GROUP
        s = _qk(qs, kcat).reshape(heads, SW_BLOCK, kcat.shape[0])
        prev_ok = (kj >= qi) & (bi > 0)
        next_ok = (kj <= qi) & (bi < n_blocks - 1)
        s = jnp.concatenate([
            jnp.where(prev_ok[None], s[:, :, 0:SW_BLOCK], NEG_INF),
            s[:, :, SW_BLOCK:2 * SW_BLOCK],
            jnp.where(next_ok[None], s[:, :, 2 * SW_BLOCK:3 * SW_BLOCK], NEG_INF),
            s[:, :, 3 * SW_BLOCK:]], axis=-1)
        o = _softmax_pv(s, vcat, sink)
        base = A_WIDTH + NA_WIDTH
        for j, slab in enumerate(_merge_sw_o(o, SW_BLOCK)):
            y_scr[pl.ds(qstart, SW_BLOCK), base + j * LANES:base + (j + 1) * LANES] = slab.astype(BF16)
        return carry

    lax.fori_loop(0, tq // SW_BLOCK, sw_block, 0)

    o_ref[0] = _out_proj_norm(y_scr[...], x_ref[0], mod_ref[0, 2:3, :], wout_ref[...],
                              lng_ref[...], lnb_ref[...], alpha)


def _mixer(u, v, naq, nak, nav, swq, swk, swv, cnak, cnav, cswk, cswv,
           ws_all, bsm, bias_tab, sink, x, mod6, wout, ln_g, ln_b, *, tq, alpha):
    b, s, d = x.shape
    lc = cnak.shape[1]
    tile = lambda w: pl.BlockSpec((1, tq, w), lambda i, j: (i, j, 0))
    full = lambda n, w: pl.BlockSpec((1, n, w), lambda i, j: (i, 0, 0))
    const = lambda shape: pl.BlockSpec(shape, lambda i, j: (0,) * len(shape))
    return pl.pallas_call(
        functools.partial(_mixer_kernel, tq=tq, seq=s, alpha=alpha),
        grid=(b, s // tq),
        in_specs=[tile(A_WIDTH), tile(A_WIDTH),
                  tile(NA_WIDTH), full(s, NA_WIDTH), full(s, NA_WIDTH),
                  tile(SW_Q_WIDTH), full(s, SW_KV_WIDTH), full(s, SW_KV_WIDTH),
                  full(lc, NA_WIDTH), full(lc, NA_WIDTH), full(lc, SW_KV_WIDTH), full(lc, SW_KV_WIDTH),
                  const(ws_all.shape), const(bsm.shape), const(bias_tab.shape), const(sink.shape),
                  tile(d), pl.BlockSpec((1, 6, d), lambda i, j: (i, 0, 0)),
                  const(wout.shape), const((1, d)), const((1, d))],
        out_specs=tile(d),
        out_shape=jax.ShapeDtypeStruct((b, s, d), F32),
        scratch_shapes=[pltpu.VMEM((tq, d), BF16)],
        compiler_params=_params(2),
        name="mixer",
    )(u, v, naq, nak, nav, swq, swk, swv, cnak, cnav, cswk, cswv,
      ws_all, bsm, bias_tab, sink, x, mod6, wout, ln_g, ln_b)


def _ctx_mixer_kernel(u_ref, v_ref, naq_ref, nak_ref, nav_ref, swq_ref, swk_ref, swv_ref,
                      ws_ref, bsm_ref, sink_ref, x_ref, mod_ref, wout_ref, lng_ref, lnb_ref,
                      o_ref, y_scr, *, lc, alpha):
    ws_all = ws_ref[...]
    bsm = bsm_ref[...]
    for c in range(lc // CHUNK):
        sl = slice(c * CHUNK, (c + 1) * CHUNK)
        y_scr[sl, 0:A_WIDTH] = _gmlp_chunk(u_ref[0, sl, :], v_ref[0, sl, :], ws_all, bsm).astype(BF16)

    s_na = _qk(_stack_na_q(naq_ref[0]), nak_ref[0])
    o_na = _softmax_pv(s_na.reshape(1, NA_HEADS * lc, lc), nav_ref[0])
    y_scr[:, A_WIDTH:A_WIDTH + NA_WIDTH] = _merge_na_o(o_na, lc).astype(BF16)

    heads = 2 * SW_GROUP
    s_sw = _qk(_stack_sw_q(swq_ref[0], lc), swk_ref[0]).reshape(heads, lc, lc)
    o_sw = _softmax_pv(s_sw, swv_ref[0], sink_ref[...])
    base = A_WIDTH + NA_WIDTH
    for j, slab in enumerate(_merge_sw_o(o_sw, lc)):
        y_scr[:, base + j * LANES:base + (j + 1) * LANES] = slab.astype(BF16)

    o_ref[0] = _out_proj_norm(y_scr[...], x_ref[0], mod_ref[0, 2:3, :], wout_ref[...],
                              lng_ref[...], lnb_ref[...], alpha)


def _ctx_mixer(u, v, naq, nak, nav, swq, swk, swv, ws_all, bsm, sink, x, mod6, wout, ln_g, ln_b, *, alpha):
    b, lc, d = x.shape
    full = lambda w: pl.BlockSpec((1, lc, w), lambda i: (i, 0, 0))
    const = lambda shape: pl.BlockSpec(shape, lambda i: (0,) * len(shape))
    return pl.pallas_call(
        functools.partial(_ctx_mixer_kernel, lc=lc, alpha=alpha),
        grid=(b,),
        in_specs=[full(A_WIDTH), full(A_WIDTH), full(NA_WIDTH), full(NA_WIDTH), full(NA_WIDTH),
                  full(SW_Q_WIDTH), full(SW_KV_WIDTH), full(SW_KV_WIDTH),
                  const(ws_all.shape), const(bsm.shape), const(sink.shape),
                  full(d), pl.BlockSpec((1, 6, d), lambda i: (i, 0, 0)),
                  const(wout.shape), const((1, d)), const((1, d))],
        out_specs=full(d),
        out_shape=jax.ShapeDtypeStruct((b, lc, d), F32),
        scratch_shapes=[pltpu.VMEM((lc, d), BF16)],
        compiler_params=_params(1),
        name="ctx_mixer",
    )(u, v, naq, nak, nav, swq, swk, swv, ws_all, bsm, sink, x, mod6, wout, ln_g, ln_b)


def _ffn_kernel(x_ref, mod_ref, w1_ref, w2_ref, lng_ref, lnb_ref, o_ref, *, alpha):
    x = x_ref[0]
    h = (x * (1.0 + mod_ref[0, 4:5, :]) + mod_ref[0, 3:4, :]).astype(BF16)
    a = jnp.dot(h, w1_ref[...], preferred_element_type=F32)
    a = jnp.square(jnp.maximum(a, 0.0)).astype(BF16)
    y = jnp.dot(a, w2_ref[...], preferred_element_type=F32)
    o_ref[0] = _layer_norm(alpha * x + mod_ref[0, 5:6, :] * y, lng_ref[...], lnb_ref[...])


def _ffn(x, mod6, w1, w2, ln_g, ln_b, *, tm, alpha):
    b, s, d = x.shape
    tile = pl.BlockSpec((1, tm, d), lambda i, j: (i, j, 0))
    const = lambda shape: pl.BlockSpec(shape, lambda i, j: (0,) * len(shape),
                                       pipeline_mode=pl.Buffered(1))
    return pl.pallas_call(
        functools.partial(_ffn_kernel, alpha=alpha),
        grid=(b, s // tm),
        in_specs=[tile, pl.BlockSpec((1, 6, d), lambda i, j: (i, 0, 0)),
                  const(w1.shape), const(w2.shape), const((1, d)), const((1, d))],
        out_specs=tile,
        out_shape=jax.ShapeDtypeStruct((b, s, d), F32),
        compiler_params=_params(2),
        name="ffn",
    )(x, mod6, w1, w2, ln_g, ln_b)


def _sw_head_permutation():
    perm = []
    for j in range(SW_GROUP):
        for kv in range(SW_KV_HEADS):
            head = kv * SW_GROUP + j
            perm.extend(range(head * HEAD_DIM, (head + 1) * HEAD_DIM))
    return np.asarray(perm, dtype=np.int32)


def _rope_tables(seq):
    pos = jnp.arange(seq)
    row_pos, col_pos = pos // GRID_W, pos % GRID_W
    inv_freq = ROPE_BASE ** (-jnp.arange(ROPE_FREQS, dtype=F32) / ROPE_FREQS)
    ang_r = row_pos.astype(F32)[:, None] * inv_freq
    ang_c = col_pos.astype(F32)[:, None] * inv_freq
    cos_head = jnp.concatenate([jnp.cos(ang_r), jnp.cos(ang_r), jnp.cos(ang_c), jnp.cos(ang_c)], axis=-1)
    sin_head = jnp.concatenate([-jnp.sin(ang_r), jnp.sin(ang_r), -jnp.sin(ang_c), jnp.sin(ang_c)], axis=-1)
    return jnp.tile(cos_head, (1, LANES // HEAD_DIM)), jnp.tile(sin_head, (1, LANES // HEAD_DIM))


def _na_bias_table(rpb):
    cq = np.arange(GRID_W)
    cs = np.clip(cq - NA_COLS // 2, 0, GRID_W - NA_COLS)
    col_ok = (cq[None, :] >= cs[:, None]) & (cq[None, :] < cs[:, None] + NA_COLS)
    dc = np.clip(cq[None, :] - cq[:, None], -(NA_COLS - 1), NA_COLS - 1) + NA_COLS - 1
    bias_col = jnp.where(col_ok[None, None], rpb[:, :, dc], NEG_INF)
    pairs = jnp.concatenate([bias_col[:, :-1], bias_col[:, 1:]], axis=-1)
    return pairs.transpose(1, 0, 2, 3).reshape(2 * NA_ROWS - 2, NA_HEADS * GRID_W, 2 * GRID_W)


def kernel(x, c, ctx, c_ctx, w_mod, b_mod, w_in, a_ln_g, a_ln_b, a_ws, a_bs, na_rpb, sw_sink, w_out,
           ln1_g, ln1_b, w1, w2, ln2_g, ln2_b):
    depth = w_mod.shape[0]
    b, s, d = x.shape
    lc = ctx.shape[1]
    alpha = (2 * depth) ** 0.25

    rows = -(-(b + 1) // MOD_ROWS_PAD) * MOD_ROWS_PAD
    cc = jnp.concatenate([c, c_ctx[None], jnp.zeros((rows - b - 1, d), F32)], axis=0)
    mod_all = _modulation(cc, w_mod, b_mod)

    perm = _sw_head_permutation()
    q0 = 2 * A_WIDTH + 3 * NA_WIDTH
    in_cols = np.concatenate([np.arange(q0), q0 + perm, np.arange(q0 + SW_Q_WIDTH, IN_WIDTH)])
    out_rows = np.concatenate([np.arange(A_WIDTH + NA_WIDTH), A_WIDTH + NA_WIDTH + perm])
    cos, sin = _rope_tables(s)

    xc = ctx
    for layer in range(depth):
        mod6 = mod_all[layer, :b].reshape(b, 6, d)
        modc6 = jnp.broadcast_to(mod_all[layer, b].reshape(1, 6, d), (b, 6, d))
        w_in_l = w_in[layer][:, in_cols].astype(BF16)
        w_out_l = w_out[layer][out_rows, :].astype(BF16)
        w1_l = w1[layer].astype(BF16)
        w2_l = w2[layer].astype(BF16)
        lng = a_ln_g[layer].reshape(1, A_WIDTH)
        lnb = a_ln_b[layer].reshape(1, A_WIDTH)
        ws_all = a_ws[layer].reshape(A_GROUPS * CHUNK, CHUNK).astype(BF16)
        bsm = jnp.repeat(a_bs[layer].T, HEAD_DIM, axis=1)
        bias_tab = _na_bias_table(na_rpb[layer])
        sink = sw_sink[layer].reshape(SW_KV_HEADS, SW_GROUP).T.reshape(2 * SW_GROUP, 1, 1)
        l1g, l1b = ln1_g[layer].reshape(1, d), ln1_b[layer].reshape(1, d)
        l2g, l2b = ln2_g[layer].reshape(1, d), ln2_b[layer].reshape(1, d)

        u, v, naq, nak, nav, swq, swk, swv = _in_projection(
            x, mod6, w_in_l, lng, lnb, cos, sin, rope=True, tm=512)
        cu, cv, cnaq, cnak, cnav, cswq, cswk, cswv = _in_projection(
            xc, modc6, w_in_l, lng, lnb, cos, sin, rope=False, tm=lc)

        x_mid = _mixer(u, v, naq, nak, nav, swq, swk, swv, cnak, cnav, cswk, cswv,
                       ws_all, bsm, bias_tab, sink, x, mod6, w_out_l, l1g, l1b, tq=512, alpha=alpha)
        x = _ffn(x_mid, mod6, w1_l, w2_l, l2g, l2b, tm=512, alpha=alpha)

        if layer < depth - 1:
            xc_mid = _ctx_mixer(cu, cv, cnaq, cnak, cnav, cswq, cswk, cswv, ws_all, bsm, sink,
                                xc, modc6, w_out_l, l1g, l1b, alpha=alpha)
            xc = _ffn(xc_mid, modc6, w1_l, w2_l, l2g, l2b, tm=lc, alpha=alpha)
    return x
```

```python
import functools
import math

import numpy as np
import jax
import jax.numpy as jnp
from jax import lax
from jax.experimental import pallas as pl
from jax.experimental.pallas import tpu as pltpu

F32 = jnp.float32
BF16 = jnp.bfloat16

D_MODEL = 1024
HEAD_DIM = 64
GRID_W = 64
CHUNK = 128
A_GROUPS = 4
A_WIDTH = 256
NA_HEADS = 4
NA_WIDTH = 256
NA_ROWS = 8
NA_COLS = 16
SW_KV_HEADS = 2
SW_GROUP = 4
SW_HEADS = SW_KV_HEADS * SW_GROUP
SW_Q_WIDTH = 512
SW_KV_WIDTH = 128
SW_BLOCK = 128
IN_WIDTH = 2048
VT_WIDTH = NA_WIDTH + SW_KV_WIDTH
ROPE_BASE = 10000.0
ROPE_FREQS = 16
LN_EPS = 1e-5
NEG_INF = -1e30
LOG2E = math.log2(math.e)
Q_SCALE = HEAD_DIM ** -0.5 * LOG2E

LANES = 128
MOD_ROWS_PAD = 8
VMEM_LIMIT = 56 * 1024 * 1024


def _layer_norm(x, g, b):
    mu = jnp.mean(x, axis=-1, keepdims=True)
    var = jnp.mean(jnp.square(x - mu), axis=-1, keepdims=True)
    return (x - mu) * lax.rsqrt(var + LN_EPS) * g + b


def _params(n_grid):
    return pltpu.CompilerParams(dimension_semantics=("arbitrary",) * n_grid,
                                vmem_limit_bytes=VMEM_LIMIT)


def _nt_dot(a, b):
    return lax.dot_general(a, b, (((1,), (1,)), ((), ())), preferred_element_type=F32)


def _mod_kernel(c_ref, w_ref, b_ref, o_ref):
    a = jax.nn.silu(c_ref[...]).astype(BF16)
    o_ref[0] = jnp.dot(a, w_ref[0].astype(BF16), preferred_element_type=F32) + b_ref[0]


def _modulation(cc, w_mod, b_mod, tn=1024):
    depth, d, n = w_mod.shape
    rows = cc.shape[0]
    return pl.pallas_call(
        _mod_kernel,
        grid=(depth, n // tn),
        in_specs=[pl.BlockSpec((rows, d), lambda l, j: (0, 0)),
                  pl.BlockSpec((1, d, tn), lambda l, j: (l, 0, j)),
                  pl.BlockSpec((1, 1, tn), lambda l, j: (l, 0, j))],
        out_specs=pl.BlockSpec((1, rows, tn), lambda l, j: (l, 0, j)),
        out_shape=jax.ShapeDtypeStruct((depth, rows, n), F32),
        compiler_params=_params(2),
        name="modulation",
    )(cc, w_mod, b_mod.reshape(depth, 1, n))


def _inproj_kernel(x_ref, mod_ref, w_ref, wvt_ref, lng_ref, lnb_ref, cos_ref, sin_ref,
                   u_ref, v_ref, naq_ref, nak_ref, navt_ref, swq_ref, swk_ref, swvt_ref, *, rope):
    x = x_ref[0]
    h = (x * (1.0 + mod_ref[0, 1:2, :]) + mod_ref[0, 0:1, :]).astype(BF16)

    def proj(lo, hi):
        return jnp.dot(h, w_ref[:, lo:hi], preferred_element_type=F32)

    if rope:
        cos = cos_ref[...]
        sin = sin_ref[...]
        first = (lax.broadcasted_iota(jnp.int32, (1, LANES), 1) % 32) < 16

        def rot(t):
            partner = jnp.where(first, pltpu.roll(t, LANES - 16, axis=1), pltpu.roll(t, 16, axis=1))
            return t * cos + partner * sin
    else:
        def rot(t):
            return t

    u_ref[0] = jax.nn.gelu(proj(0, 256)).astype(BF16)
    v = jax.nn.gelu(proj(256, 512))
    v_ref[0] = _layer_norm(v, lng_ref[...], lnb_ref[...]).astype(BF16)
    naq_ref[0] = (proj(512, 768) * Q_SCALE).astype(BF16)
    nak_ref[0] = proj(768, 1024).astype(BF16)
    q = proj(1024, 1536)
    for j in range(SW_Q_WIDTH // LANES):
        sl = slice(j * LANES, (j + 1) * LANES)
        swq_ref[0, :, sl] = (rot(q[:, sl]) * Q_SCALE).astype(BF16)
    swk_ref[0] = rot(proj(1536, 1664)).astype(BF16)
    vt = _nt_dot(wvt_ref[...], h)
    navt_ref[0] = vt[:NA_WIDTH].astype(BF16)
    swvt_ref[0] = vt[NA_WIDTH:].astype(BF16)


def _in_projection(x, mod6, w_main, w_vt, ln_g, ln_b, cos, sin, *, rope, tm):
    b, s, d = x.shape
    tile = lambda w: pl.BlockSpec((1, tm, w), lambda i, j: (i, j, 0))
    tile_t = lambda w: pl.BlockSpec((1, w, tm), lambda i, j: (i, 0, j))
    const = lambda shape: pl.BlockSpec(shape, lambda i, j: (0,) * len(shape))
    nat = lambda w: jax.ShapeDtypeStruct((b, s, w), BF16)
    tra = lambda w: jax.ShapeDtypeStruct((b, w, s), BF16)
    return pl.pallas_call(
        functools.partial(_inproj_kernel, rope=rope),
        grid=(b, s // tm),
        in_specs=[tile(d),
                  pl.BlockSpec((1, 6, d), lambda i, j: (i, 0, 0)),
                  const(w_main.shape), const(w_vt.shape),
                  const((1, A_WIDTH)), const((1, A_WIDTH)),
                  pl.BlockSpec((tm, LANES), lambda i, j: (j, 0)),
                  pl.BlockSpec((tm, LANES), lambda i, j: (j, 0))],
        out_specs=[tile(A_WIDTH), tile(A_WIDTH), tile(NA_WIDTH), tile(NA_WIDTH), tile_t(NA_WIDTH),
                   tile(SW_Q_WIDTH), tile(SW_KV_WIDTH), tile_t(SW_KV_WIDTH)],
        out_shape=[nat(A_WIDTH), nat(A_WIDTH), nat(NA_WIDTH), nat(NA_WIDTH), tra(NA_WIDTH),
                   nat(SW_Q_WIDTH), nat(SW_KV_WIDTH), tra(SW_KV_WIDTH)],
        compiler_params=_params(2),
        name="in_projection_rope" if rope else "in_projection",
    )(x, mod6, w_main, w_vt, ln_g, ln_b, cos, sin)


def _gmlp_chunk(u, v, ws_all, bsm):
    m = jnp.dot(ws_all, v, preferred_element_type=F32)
    group = lax.broadcasted_iota(jnp.int32, (1, A_WIDTH), 1) // HEAD_DIM
    mixed = m[0:CHUNK]
    for g in range(1, A_GROUPS):
        mixed = jnp.where(group == g, m[g * CHUNK:(g + 1) * CHUNK], mixed)
    return u.astype(F32) * (mixed + bsm)


def _stack_na_q(q):
    head = lax.broadcasted_iota(jnp.int32, (1, NA_WIDTH), 1) // HEAD_DIM
    zero = jnp.zeros_like(q)
    return jnp.concatenate([jnp.where(head == h, q, zero) for h in range(NA_HEADS)], axis=0)


def _stack_sw_q(q):
    low = lax.broadcasted_iota(jnp.int32, (1, LANES), 1) < HEAD_DIM
    parts = []
    for j in range(SW_GROUP):
        qj = q[:, j * LANES:(j + 1) * LANES]
        zero = jnp.zeros_like(qj)
        parts.append(jnp.where(low, qj, zero))
        parts.append(jnp.where(low, zero, qj))
    return jnp.concatenate(parts, axis=0)


def _softmax_t(s_t, sink=None):
    m = jnp.max(s_t, axis=0, keepdims=True)
    if sink is not None:
        m = jnp.maximum(m, sink)
    e = jnp.exp2(s_t - m)
    l = jnp.sum(e, axis=0, keepdims=True)
    if sink is not None:
        l = l + jnp.exp2(sink - m)
    return e.astype(BF16), 1.0 / l


def _na_attend(q, kcat, v_t, bias_fn):
    n = q.shape[0]
    s_t = bias_fn(_nt_dot(kcat, _stack_na_q(q)))
    e, inv_l = _softmax_t(s_t)
    o = (jnp.dot(v_t, e, preferred_element_type=F32) * inv_l).T
    head = lax.broadcasted_iota(jnp.int32, (1, NA_WIDTH), 1) // HEAD_DIM
    out = o[0:n]
    for h in range(1, NA_HEADS):
        out = jnp.where(head == h, o[h * n:(h + 1) * n], out)
    return out


def _sw_attend(q, kcat, v_t, sink, mask_fn):
    n = q.shape[0]
    s_t = mask_fn(_nt_dot(kcat, _stack_sw_q(q)))
    e, inv_l = _softmax_t(s_t, sink)
    slabs = []
    for j in range(SW_GROUP):
        cols = slice(2 * j * n, (2 * j + 2) * n)
        o_t = jnp.dot(v_t, e[:, cols], preferred_element_type=F32) * inv_l[:, cols]
        slab_t = jnp.concatenate([o_t[:HEAD_DIM, :n], o_t[HEAD_DIM:, n:]], axis=0)
        slabs.append(slab_t.T)
    return slabs


def _out_proj_norm(y, x, g1, wout, ln_g, ln_b, alpha):
    yo = jnp.dot(y, wout, preferred_element_type=F32)
    return _layer_norm(alpha * x + g1 * yo, ln_g, ln_b)


def _mixer_kernel(u_ref, v_ref, naq_ref, nak_ref, navt_ref, swq_ref, swk_ref, swvt_ref,
                  cnak_ref, cnavt_ref, cswk_ref, cswvt_ref,
                  ws_ref, bsm_ref, bias_ref, swmask_ref, sink_ref,
                  x_ref, mod_ref, wout_ref, lng_ref, lnb_ref,
                  o_ref, y_scr, navt2_scr, *, tq, seq, alpha):
    tile = pl.program_id(1)
    rows_per_tile = tq // GRID_W
    n_rows = seq // GRID_W
    win_keys = NA_ROWS * GRID_W

    @pl.when(tile == 0)
    def _():
        navt2_scr[0] = navt_ref[0]
        navt2_scr[1, :, 0:seq - GRID_W] = navt_ref[0, :, GRID_W:seq]
        navt2_scr[1, :, seq - GRID_W:seq] = jnp.zeros((NA_WIDTH, GRID_W), BF16)

    ws_all = ws_ref[...]
    bsm = bsm_ref[...]
    for c in range(tq // CHUNK):
        sl = slice(c * CHUNK, (c + 1) * CHUNK)
        y_scr[sl, 0:A_WIDTH] = _gmlp_chunk(u_ref[0, sl, :], v_ref[0, sl, :], ws_all, bsm).astype(BF16)

    def na_row(i, carry):
        r = tile * rows_per_tile + i
        rs = jnp.clip(r - NA_ROWS // 2, 0, n_rows - NA_ROWS)
        off = r - rs
        par = rs % 2
        kstart = pl.multiple_of(rs * GRID_W, GRID_W)
        vstart = pl.multiple_of((rs - par) * GRID_W, LANES)
        qstart = pl.multiple_of(i * GRID_W, GRID_W)
        kcat = jnp.concatenate([nak_ref[0, pl.ds(kstart, win_keys), :], cnak_ref[0]], axis=0)
        v_t = jnp.concatenate([navt2_scr[par, :, pl.ds(vstart, win_keys)], cnavt_ref[0]], axis=1)

        def add_bias(s_t):
            parts = [s_t[j * GRID_W:(j + 1) * GRID_W] + bias_ref[j + NA_ROWS - 1 - off] for j in range(NA_ROWS)]
            return jnp.concatenate(parts + [s_t[win_keys:]], axis=0)

        out = _na_attend(naq_ref[0, pl.ds(qstart, GRID_W), :], kcat, v_t, add_bias)
        y_scr[pl.ds(qstart, GRID_W), A_WIDTH:A_WIDTH + NA_WIDTH] = out.astype(BF16)
        return carry

    lax.fori_loop(0, rows_per_tile, na_row, 0, unroll=True)

    n_blocks = seq // SW_BLOCK
    sink = sink_ref[...]

    def sw_block(i, carry):
        bi = tile * (tq // SW_BLOCK) + i
        t0 = bi * SW_BLOCK
        qstart = pl.multiple_of(i * SW_BLOCK, SW_BLOCK)
        p0 = pl.multiple_of(jnp.maximum(t0 - SW_BLOCK, 0), SW_BLOCK)
        c0 = pl.multiple_of(t0, SW_BLOCK)
        n0 = pl.multiple_of(jnp.minimum(t0 + SW_BLOCK, seq - SW_BLOCK), SW_BLOCK)
        kcat = jnp.concatenate([swk_ref[0, pl.ds(p0, SW_BLOCK), :], swk_ref[0, pl.ds(c0, SW_BLOCK), :],
                                swk_ref[0, pl.ds(n0, SW_BLOCK), :], cswk_ref[0]], axis=0)
        v_t = jnp.concatenate([swvt_ref[0, :, pl.ds(p0, SW_BLOCK)], swvt_ref[0, :, pl.ds(c0, SW_BLOCK)],
                               swvt_ref[0, :, pl.ds(n0, SW_BLOCK)], cswvt_ref[0]], axis=1)
        prev_mask = swmask_ref[jnp.where(bi > 0, 0, 2)]
        next_mask = swmask_ref[jnp.where(bi < n_blocks - 1, 1, 2)]

        def add_mask(s_t):
            return jnp.concatenate([s_t[0:SW_BLOCK] + prev_mask, s_t[SW_BLOCK:2 * SW_BLOCK],
                                    s_t[2 * SW_BLOCK:3 * SW_BLOCK] + next_mask, s_t[3 * SW_BLOCK:]], axis=0)

        slabs = _sw_attend(swq_ref[0, pl.ds(qstart, SW_BLOCK), :], kcat, v_t, sink, add_mask)
        base = A_WIDTH + NA_WIDTH
        for j, slab in enumerate(slabs):
            y_scr[pl.ds(qstart, SW_BLOCK), base + j * LANES:base + (j + 1) * LANES] = slab.astype(BF16)
        return carry

    lax.fori_loop(0, tq // SW_BLOCK, sw_block, 0, unroll=True)

    o_ref[0] = _out_proj_norm(y_scr[...], x_ref[0], mod_ref[0, 2:3, :], wout_ref[...],
                              lng_ref[...], lnb_ref[...], alpha)


def _mixer(u, v, naq, nak, navt, swq, swk, swvt, cnak, cnavt, cswk, cswvt,
           ws_all, bsm, bias_tab, sw_mask, sink, x, mod6, wout, ln_g, ln_b, *, tq, alpha):
    b, s, d = x.shape
    lc = cnak.shape[1]
    tile = lambda w: pl.BlockSpec((1, tq, w), lambda i, j: (i, j, 0))
    full = lambda n, w: pl.BlockSpec((1, n, w), lambda i, j: (i, 0, 0))
    const = lambda shape: pl.BlockSpec(shape, lambda i, j: (0,) * len(shape))
    return pl.pallas_call(
        functools.partial(_mixer_kernel, tq=tq, seq=s, alpha=alpha),
        grid=(b, s // tq),
        in_specs=[tile(A_WIDTH), tile(A_WIDTH),
                  tile(NA_WIDTH), full(s, NA_WIDTH), full(NA_WIDTH, s),
                  tile(SW_Q_WIDTH), full(s, SW_KV_WIDTH), full(SW_KV_WIDTH, s),
                  full(lc, NA_WIDTH), full(NA_WIDTH, lc), full(lc, SW_KV_WIDTH), full(SW_KV_WIDTH, lc),
                  const(ws_all.shape), const(bsm.shape), const(bias_tab.shape), const(sw_mask.shape),
                  const(sink.shape),
                  tile(d), pl.BlockSpec((1, 6, d), lambda i, j: (i, 0, 0)),
                  const(wout.shape), const((1, d)), const((1, d))],
        out_specs=tile(d),
        out_shape=jax.ShapeDtypeStruct((b, s, d), F32),
        scratch_shapes=[pltpu.VMEM((tq, d), BF16), pltpu.VMEM((2, NA_WIDTH, s), BF16)],
        compiler_params=_params(2),
        name="mixer",
    )(u, v, naq, nak, navt, swq, swk, swvt, cnak, cnavt, cswk, cswvt,
      ws_all, bsm, bias_tab, sw_mask, sink, x, mod6, wout, ln_g, ln_b)


def _ctx_mixer_kernel(u_ref, v_ref, naq_ref, nak_ref, navt_ref, swq_ref, swk_ref, swvt_ref,
                      ws_ref, bsm_ref, sink_ref, x_ref, mod_ref, wout_ref, lng_ref, lnb_ref,
                      o_ref, y_scr, *, lc, alpha):
    ws_all = ws_ref[...]
    bsm = bsm_ref[...]
    for c in range(lc // CHUNK):
        sl = slice(c * CHUNK, (c + 1) * CHUNK)
        y_scr[sl, 0:A_WIDTH] = _gmlp_chunk(u_ref[0, sl, :], v_ref[0, sl, :], ws_all, bsm).astype(BF16)

    out = _na_attend(naq_ref[0], nak_ref[0], navt_ref[0], lambda s_t: s_t)
    y_scr[:, A_WIDTH:A_WIDTH + NA_WIDTH] = out.astype(BF16)

    slabs = _sw_attend(swq_ref[0], swk_ref[0], swvt_ref[0], sink_ref[...], lambda s_t: s_t)
    base = A_WIDTH + NA_WIDTH
    for j, slab in enumerate(slabs):
        y_scr[:, base + j * LANES:base + (j + 1) * LANES] = slab.astype(BF16)

    o_ref[0] = _out_proj_norm(y_scr[...], x_ref[0], mod_ref[0, 2:3, :], wout_ref[...],
                              lng_ref[...], lnb_ref[...], alpha)


def _ctx_mixer(u, v, naq, nak, navt, swq, swk, swvt, ws_all, bsm, sink, x, mod6, wout, ln_g, ln_b, *, alpha):
    b, lc, d = x.shape
    full = lambda n, w: pl.BlockSpec((1, n, w), lambda i: (i, 0, 0))
    const = lambda shape: pl.BlockSpec(shape, lambda i: (0,) * len(shape))
    return pl.pallas_call(
        functools.partial(_ctx_mixer_kernel, lc=lc, alpha=alpha),
        grid=(b,),
        in_specs=[full(lc, A_WIDTH), full(lc, A_WIDTH), full(lc, NA_WIDTH), full(lc, NA_WIDTH),
                  full(NA_WIDTH, lc), full(lc, SW_Q_WIDTH), full(lc, SW_KV_WIDTH), full(SW_KV_WIDTH, lc),
                  const(ws_all.shape), const(bsm.shape), const(sink.shape),
                  full(lc, d), pl.BlockSpec((1, 6, d), lambda i: (i, 0, 0)),
                  const(wout.shape), const((1, d)), const((1, d))],
        out_specs=full(lc, d),
        out_shape=jax.ShapeDtypeStruct((b, lc, d), F32),
        scratch_shapes=[pltpu.VMEM((lc, d), BF16)],
        compiler_params=_params(1),
        name="ctx_mixer",
    )(u, v, naq, nak, navt, swq, swk, swvt, ws_all, bsm, sink, x, mod6, wout, ln_g, ln_b)


def _ffn_kernel(x_ref, mod_ref, w1_ref, w2_ref, lng_ref, lnb_ref, o_ref, *, alpha):
    x = x_ref[0]
    h = (x * (1.0 + mod_ref[0, 4:5, :]) + mod_ref[0, 3:4, :]).astype(BF16)
    a = jnp.dot(h, w1_ref[...], preferred_element_type=F32)
    a = jnp.square(jnp.maximum(a, 0.0)).astype(BF16)
    y = jnp.dot(a, w2_ref[...], preferred_element_type=F32)
    o_ref[0] = _layer_norm(alpha * x + mod_ref[0, 5:6, :] * y, lng_ref[...], lnb_ref[...])


def _ffn(x, mod6, w1, w2, ln_g, ln_b, *, tm, alpha):
    b, s, d = x.shape
    tile = pl.BlockSpec((1, tm, d), lambda i, j: (i, j, 0))
    const = lambda shape: pl.BlockSpec(shape, lambda i, j: (0,) * len(shape),
                                       pipeline_mode=pl.Buffered(1))
    return pl.pallas_call(
        functools.partial(_ffn_kernel, alpha=alpha),
        grid=(b, s // tm),
        in_specs=[tile, pl.BlockSpec((1, 6, d), lambda i, j: (i, 0, 0)),
                  const(w1.shape), const(w2.shape), const((1, d)), const((1, d))],
        out_specs=tile,
        out_shape=jax.ShapeDtypeStruct((b, s, d), F32),
        compiler_params=_params(2),
        name="ffn",
    )(x, mod6, w1, w2, ln_g, ln_b)


def _sw_head_permutation():
    perm = []
    for j in range(SW_GROUP):
        for kv in range(SW_KV_HEADS):
            head = kv * SW_GROUP + j
            perm.extend(range(head * HEAD_DIM, (head + 1) * HEAD_DIM))
    return np.asarray(perm, dtype=np.int32)


def _rope_tables(seq):
    pos = jnp.arange(seq)
    row_pos, col_pos = pos // GRID_W, pos % GRID_W
    inv_freq = ROPE_BASE ** (-jnp.arange(ROPE_FREQS, dtype=F32) / ROPE_FREQS)
    ang_r = row_pos.astype(F32)[:, None] * inv_freq
    ang_c = col_pos.astype(F32)[:, None] * inv_freq
    cos_head = jnp.concatenate([jnp.cos(ang_r), jnp.cos(ang_r), jnp.cos(ang_c), jnp.cos(ang_c)], axis=-1)
    sin_head = jnp.concatenate([-jnp.sin(ang_r), jnp.sin(ang_r), -jnp.sin(ang_c), jnp.sin(ang_c)], axis=-1)
    return jnp.tile(cos_head, (1, LANES // HEAD_DIM)), jnp.tile(sin_head, (1, LANES // HEAD_DIM))


def _na_bias_table(rpb):
    cq = np.arange(GRID_W)
    cs = np.clip(cq - NA_COLS // 2, 0, GRID_W - NA_COLS)
    col_ok = (cq[None, :] >= cs[:, None]) & (cq[None, :] < cs[:, None] + NA_COLS)
    dc = np.clip(cq[None, :] - cq[:, None], -(NA_COLS - 1), NA_COLS - 1) + NA_COLS - 1
    bias_col = jnp.where(col_ok[None, None], rpb[:, :, dc] * LOG2E, NEG_INF)
    return bias_col.transpose(1, 3, 0, 2).reshape(2 * NA_ROWS - 1, GRID_W, NA_HEADS * GRID_W)


def _sw_mask_table():
    kj = np.arange(SW_BLOCK)[:, None]
    qi = np.arange(SW_BLOCK)[None, :]
    prev = np.where(kj >= qi, 0.0, NEG_INF)
    nxt = np.where(kj <= qi, 0.0, NEG_INF)
    none = np.full((SW_BLOCK, SW_BLOCK), NEG_INF)
    return jnp.asarray(np.stack([np.tile(m, (1, SW_HEADS)) for m in (prev, nxt, none)]), F32)


def kernel(x, c, ctx, c_ctx, w_mod, b_mod, w_in, a_ln_g, a_ln_b, a_ws, a_bs, na_rpb, sw_sink, w_out,
           ln1_g, ln1_b, w1, w2, ln2_g, ln2_b):
    depth = w_mod.shape[0]
    b, s, d = x.shape
    lc = ctx.shape[1]
    alpha = (2 * depth) ** 0.25
    tq = 512

    rows = -(-(b + 1) // MOD_ROWS_PAD) * MOD_ROWS_PAD
    cc = jnp.concatenate([c, c_ctx[None], jnp.zeros((rows - b - 1, d), F32)], axis=0)
    mod_all = _modulation(cc, w_mod, b_mod)

    perm = _sw_head_permutation()
    q0 = 2 * A_WIDTH + 3 * NA_WIDTH
    nav0 = 2 * A_WIDTH + 2 * NA_WIDTH
    main_cols = np.concatenate([np.arange(nav0), q0 + perm, np.arange(q0 + SW_Q_WIDTH, q0 + SW_Q_WIDTH + SW_KV_WIDTH)])
    vt_cols = np.concatenate([np.arange(nav0, nav0 + NA_WIDTH), np.arange(IN_WIDTH - SW_KV_WIDTH, IN_WIDTH)])
    out_rows = np.concatenate([np.arange(A_WIDTH + NA_WIDTH), A_WIDTH + NA_WIDTH + perm])
    cos, sin = _rope_tables(s)
    sw_mask = _sw_mask_table()

    xc = ctx
    for layer in range(depth):
        mod6 = mod_all[layer, :b].reshape(b, 6, d)
        modc6 = jnp.broadcast_to(mod_all[layer, b].reshape(1, 6, d), (b, 6, d))
        w_main = w_in[layer][:, main_cols].astype(BF16)
        w_vt = w_in[layer][:, vt_cols].T.astype(BF16)
        w_out_l = w_out[layer][out_rows, :].astype(BF16)
        w1_l = w1[layer].astype(BF16)
        w2_l = w2[layer].astype(BF16)
        lng = a_ln_g[layer].reshape(1, A_WIDTH)
        lnb = a_ln_b[layer].reshape(1, A_WIDTH)
        ws_all = a_ws[layer].reshape(A_GROUPS * CHUNK, CHUNK).astype(BF16)
        bsm = jnp.repeat(a_bs[layer].T, HEAD_DIM, axis=1)
        bias_tab = _na_bias_table(na_rpb[layer])
        sink8 = sw_sink[layer].reshape(SW_KV_HEADS, SW_GROUP).T.reshape(SW_HEADS) * LOG2E
        sink_blk = jnp.repeat(sink8, SW_BLOCK)[None, :]
        sink_ctx = jnp.repeat(sink8, lc)[None, :]
        l1g, l1b = ln1_g[layer].reshape(1, d), ln1_b[layer].reshape(1, d)
        l2g, l2b = ln2_g[layer].reshape(1, d), ln2_b[layer].reshape(1, d)

        u, v, naq, nak, navt, swq, swk, swvt = _in_projection(
            x, mod6, w_main, w_vt, lng, lnb, cos, sin, rope=True, tm=512)
        cu, cv, cnaq, cnak, cnavt, cswq, cswk, cswvt = _in_projection(
            xc, modc6, w_main, w_vt, lng, lnb, cos, sin, rope=False, tm=lc)

        x_mid = _mixer(u, v, naq, nak, navt, swq, swk, swvt, cnak, cnavt, cswk, cswvt,
                       ws_all, bsm, bias_tab, sw_mask, sink_blk, x, mod6, w_out_l, l1g, l1b, tq=tq, alpha=alpha)
        x = _ffn(x_mid, mod6, w1_l, w2_l, l2g, l2b, tm=512, alpha=alpha)

        if layer < depth - 1:
            xc_mid = _ctx_mixer(cu, cv, cnaq, cnak, cnavt, cswq, cswk, cswvt, ws_all, bsm, sink_ctx,
                                xc, modc6, w_out_l, l1g, l1b, alpha=alpha)
            xc = _ffn(xc_mid, modc6, w1_l, w2_l, l2g, l2b, tm=lc, alpha=alpha)
    return x
```

```python
import functools
import math

import numpy as np
import jax
import jax.numpy as jnp
from jax import lax
from jax.experimental import pallas as pl
from jax.experimental.pallas import tpu as pltpu

F32 = jnp.float32
BF16 = jnp.bfloat16

D_MODEL = 1024
HEAD_DIM = 64
GRID_W = 64
CHUNK = 128
A_GROUPS = 4
A_WIDTH = 256
NA_HEADS = 4
NA_WIDTH = 256
NA_ROWS = 8
NA_COLS = 16
SW_KV_HEADS = 2
SW_GROUP = 4
SW_HEADS = SW_KV_HEADS * SW_GROUP
SW_Q_WIDTH = 512
SW_KV_WIDTH = 128
SW_BLOCK = 128
IN_WIDTH = 2048
VT_WIDTH = NA_WIDTH + SW_KV_WIDTH
ONES_ROWS = 16
NA_VT_ROWS = NA_WIDTH + ONES_ROWS
SW_VT_ROWS = SW_KV_WIDTH + ONES_ROWS
ROPE_BASE = 10000.0
ROPE_FREQS = 16
LN_EPS = 1e-5
NEG_INF = -1e30
LOG2E = math.log2(math.e)
Q_SCALE = HEAD_DIM ** -0.5 * LOG2E

LANES = 128
MOD_ROWS_PAD = 8
VMEM_LIMIT = 56 * 1024 * 1024


def _layer_norm(x, g, b):
    mu = jnp.mean(x, axis=-1, keepdims=True)
    var = jnp.mean(jnp.square(x - mu), axis=-1, keepdims=True)
    return (x - mu) * lax.rsqrt(var + LN_EPS) * g + b


def _params(n_grid):
    return pltpu.CompilerParams(dimension_semantics=("arbitrary",) * n_grid,
                                vmem_limit_bytes=VMEM_LIMIT)


def _nt_dot(a, b):
    return lax.dot_general(a, b, (((1,), (1,)), ((), ())), preferred_element_type=F32)


def _mod_kernel(c_ref, w_ref, b_ref, o_ref):
    a = jax.nn.silu(c_ref[...]).astype(BF16)
    o_ref[0] = jnp.dot(a, w_ref[0].astype(BF16), preferred_element_type=F32) + b_ref[0]


def _modulation(cc, w_mod, b_mod, tn=1024):
    depth, d, n = w_mod.shape
    rows = cc.shape[0]
    return pl.pallas_call(
        _mod_kernel,
        grid=(depth, n // tn),
        in_specs=[pl.BlockSpec((rows, d), lambda l, j: (0, 0)),
                  pl.BlockSpec((1, d, tn), lambda l, j: (l, 0, j)),
                  pl.BlockSpec((1, 1, tn), lambda l, j: (l, 0, j))],
        out_specs=pl.BlockSpec((1, rows, tn), lambda l, j: (l, 0, j)),
        out_shape=jax.ShapeDtypeStruct((depth, rows, n), F32),
        compiler_params=_params(2),
        name="modulation",
    )(cc, w_mod, b_mod.reshape(depth, 1, n))


def _inproj_kernel(x_ref, mod_ref, w_ref, wvt_ref, lng_ref, lnb_ref, cos_ref, sin_ref,
                   u_ref, v_ref, naq_ref, nak_ref, navt_ref, swq_ref, swk_ref, swvt_ref, *, rope):
    x = x_ref[0]
    h = (x * (1.0 + mod_ref[0, 1:2, :]) + mod_ref[0, 0:1, :]).astype(BF16)

    def proj(lo, hi):
        return jnp.dot(h, w_ref[:, lo:hi], preferred_element_type=F32)

    if rope:
        cos = cos_ref[...]
        sin = sin_ref[...]
        first = (lax.broadcasted_iota(jnp.int32, (1, LANES), 1) % 32) < 16

        def rot(t):
            partner = jnp.where(first, pltpu.roll(t, LANES - 16, axis=1), pltpu.roll(t, 16, axis=1))
            return t * cos + partner * sin
    else:
        def rot(t):
            return t

    u_ref[0] = jax.nn.gelu(proj(0, 256)).astype(BF16)
    v = jax.nn.gelu(proj(256, 512))
    v_ref[0] = _layer_norm(v, lng_ref[...], lnb_ref[...]).astype(BF16)
    naq_ref[0] = (proj(512, 768) * Q_SCALE).astype(BF16)
    nak_ref[0] = proj(768, 1024).astype(BF16)
    qk = proj(1024, 1664)
    for j in range(SW_Q_WIDTH // LANES):
        sl = slice(j * LANES, (j + 1) * LANES)
        swq_ref[0, :, sl] = (rot(qk[:, sl]) * Q_SCALE).astype(BF16)
    swk_ref[0] = rot(qk[:, SW_Q_WIDTH:]).astype(BF16)
    vt = _nt_dot(wvt_ref[...], h)
    ones = jnp.ones((ONES_ROWS, vt.shape[1]), BF16)
    navt_ref[0, :NA_WIDTH, :] = vt[:NA_WIDTH].astype(BF16)
    navt_ref[0, NA_WIDTH:, :] = ones
    swvt_ref[0, :SW_KV_WIDTH, :] = vt[NA_WIDTH:].astype(BF16)
    swvt_ref[0, SW_KV_WIDTH:, :] = ones


def _in_projection(x, mod6, w_main, w_vt, ln_g, ln_b, cos, sin, *, rope, tm):
    b, s, d = x.shape
    tile = lambda w: pl.BlockSpec((1, tm, w), lambda i, j: (i, j, 0))
    tile_t = lambda w: pl.BlockSpec((1, w, tm), lambda i, j: (i, 0, j))
    const = lambda shape: pl.BlockSpec(shape, lambda i, j: (0,) * len(shape))
    nat = lambda w: jax.ShapeDtypeStruct((b, s, w), BF16)
    tra = lambda w: jax.ShapeDtypeStruct((b, w, s), BF16)
    return pl.pallas_call(
        functools.partial(_inproj_kernel, rope=rope),
        grid=(b, s // tm),
        in_specs=[tile(d),
                  pl.BlockSpec((1, 6, d), lambda i, j: (i, 0, 0)),
                  const(w_main.shape), const(w_vt.shape),
                  const((1, A_WIDTH)), const((1, A_WIDTH)),
                  pl.BlockSpec((tm, LANES), lambda i, j: (j, 0)),
                  pl.BlockSpec((tm, LANES), lambda i, j: (j, 0))],
        out_specs=[tile(A_WIDTH), tile(A_WIDTH), tile(NA_WIDTH), tile(NA_WIDTH), tile_t(NA_VT_ROWS),
                   tile(SW_Q_WIDTH), tile(SW_KV_WIDTH), tile_t(SW_VT_ROWS)],
        out_shape=[nat(A_WIDTH), nat(A_WIDTH), nat(NA_WIDTH), nat(NA_WIDTH), tra(NA_VT_ROWS),
                   nat(SW_Q_WIDTH), nat(SW_KV_WIDTH), tra(SW_VT_ROWS)],
        compiler_params=_params(2),
        name="in_projection_rope" if rope else "in_projection",
    )(x, mod6, w_main, w_vt, ln_g, ln_b, cos, sin)


def _gmlp_chunk(u, v, ws_all, bsm):
    m = jnp.dot(ws_all, v, preferred_element_type=F32)
    group = lax.broadcasted_iota(jnp.int32, (1, A_WIDTH), 1) // HEAD_DIM
    mixed = m[0:CHUNK]
    for g in range(1, A_GROUPS):
        mixed = jnp.where(group == g, m[g * CHUNK:(g + 1) * CHUNK], mixed)
    return u.astype(F32) * (mixed + bsm)


def _stack_na_q(q):
    head = lax.broadcasted_iota(jnp.int32, (1, NA_WIDTH), 1) // HEAD_DIM
    zero = jnp.zeros_like(q)
    return jnp.concatenate([jnp.where(head == h, q, zero) for h in range(NA_HEADS)], axis=0)


def _stack_sw_q(q):
    low = lax.broadcasted_iota(jnp.int32, (1, LANES), 1) < HEAD_DIM
    slabs = [q[:, j * LANES:(j + 1) * LANES] for j in range(SW_GROUP)]
    zero = jnp.zeros_like(slabs[0])
    return jnp.concatenate([jnp.where(low, qj, zero) for qj in slabs]
                           + [jnp.where(low, zero, qj) for qj in slabs], axis=0)


def _exp_t(s_t, sink=None):
    m = jnp.max(s_t, axis=0, keepdims=True)
    if sink is not None:
        m = jnp.maximum(m, sink)
    return jnp.exp2((s_t - m).astype(BF16)), m


def _na_scores(q, kcat):
    return _nt_dot(kcat, _stack_na_q(q))


def _na_finish(e, v_t):
    n = e.shape[1] // NA_HEADS
    o_t = jnp.dot(v_t, e, preferred_element_type=F32)
    o = (o_t[:NA_WIDTH] * (1.0 / o_t[NA_WIDTH:NA_WIDTH + 1])).T
    head = lax.broadcasted_iota(jnp.int32, (1, NA_WIDTH), 1) // HEAD_DIM
    out = o[0:n]
    for h in range(1, NA_HEADS):
        out = jnp.where(head == h, o[h * n:(h + 1) * n], out)
    return out


def _sw_scores(q, kcat):
    return _nt_dot(kcat, _stack_sw_q(q))


def _sw_probs(s_t, sink):
    e, m = _exp_t(s_t, sink)
    return e, jnp.exp2(sink - m)


def _sw_finish(e, sink_e, v_t):
    n = e.shape[1] // SW_HEADS
    ones = v_t[SW_KV_WIDTH:]
    o_kv = []
    for kv in range(SW_KV_HEADS):
        cols = slice(kv * SW_GROUP * n, (kv + 1) * SW_GROUP * n)
        v_kv = jnp.concatenate([v_t[kv * HEAD_DIM:(kv + 1) * HEAD_DIM], ones], axis=0)
        o_t = jnp.dot(v_kv, e[:, cols], preferred_element_type=F32)
        inv_l = 1.0 / (o_t[HEAD_DIM:HEAD_DIM + 1] + sink_e[:, cols])
        o_kv.append(o_t[:HEAD_DIM] * inv_l)
    return [jnp.concatenate([o[:, j * n:(j + 1) * n] for o in o_kv], axis=0).T for j in range(SW_GROUP)]


def _mixer_kernel(u_ref, v_ref, naq_ref, nak_ref, navt_ref, swq_ref, swk_ref, swvt_ref,
                  cnak_ref, cnavt_ref, cswk_ref, cswvt_ref,
                  ws_ref, bsm_ref, bias_ref, swmask_ref, sink_ref,
                  y_ref, navt2_scr, *, tq, seq):
    tile = pl.program_id(1)
    rows_per_tile = tq // GRID_W
    n_rows = seq // GRID_W
    win_keys = NA_ROWS * GRID_W

    @pl.when(tile == 0)
    def _():
        navt2_scr[0] = navt_ref[0]
        navt2_scr[1, :, 0:seq - GRID_W] = navt_ref[0, :, GRID_W:seq]
        navt2_scr[1, :, seq - GRID_W:seq] = jnp.zeros((NA_VT_ROWS, GRID_W), BF16)

    ws_all = ws_ref[...]
    bsm = bsm_ref[...]

    def na_window(i):
        r = tile * rows_per_tile + i
        rs = jnp.clip(r - NA_ROWS // 2, 0, n_rows - NA_ROWS)
        return rs, r - rs

    def na_scores(i, _):
        rs, off = na_window(i)
        kstart = pl.multiple_of(rs * GRID_W, GRID_W)
        kcat = jnp.concatenate([nak_ref[0, pl.ds(kstart, win_keys), :], cnak_ref[0]], axis=0)
        s_t = _na_scores(naq_ref[0, i * GRID_W:(i + 1) * GRID_W, :], kcat)
        parts = [s_t[j * GRID_W:(j + 1) * GRID_W] + bias_ref[j + NA_ROWS - 1 - off] for j in range(NA_ROWS)]
        return jnp.concatenate(parts + [s_t[win_keys:]], axis=0)

    def na_probs(i, s_t):
        return _exp_t(s_t)[0]

    def na_finish(i, e):
        rs, _ = na_window(i)
        par = rs % 2
        vstart = pl.multiple_of((rs - par) * GRID_W, LANES)
        v_t = jnp.concatenate([navt2_scr[par, :, pl.ds(vstart, win_keys)], cnavt_ref[0]], axis=1)
        y_ref[0, i * GRID_W:(i + 1) * GRID_W, A_WIDTH:A_WIDTH + NA_WIDTH] = _na_finish(e, v_t).astype(BF16)

    n_blocks = seq // SW_BLOCK
    sink = sink_ref[...]

    def sw_starts(i):
        t0 = (tile * (tq // SW_BLOCK) + i) * SW_BLOCK
        return (pl.multiple_of(jnp.maximum(t0 - SW_BLOCK, 0), SW_BLOCK), pl.multiple_of(t0, SW_BLOCK),
                pl.multiple_of(jnp.minimum(t0 + SW_BLOCK, seq - SW_BLOCK), SW_BLOCK))

    def sw_scores(i, _):
        bi = tile * (tq // SW_BLOCK) + i
        kcat = jnp.concatenate([swk_ref[0, pl.ds(st, SW_BLOCK), :] for st in sw_starts(i)] + [cswk_ref[0]], axis=0)
        s_t = _sw_scores(swq_ref[0, i * SW_BLOCK:(i + 1) * SW_BLOCK, :], kcat)
        prev_mask = swmask_ref[jnp.where(bi > 0, 0, 2)]
        next_mask = swmask_ref[jnp.where(bi < n_blocks - 1, 1, 2)]
        return jnp.concatenate([s_t[0:SW_BLOCK] + prev_mask, s_t[SW_BLOCK:2 * SW_BLOCK],
                                s_t[2 * SW_BLOCK:3 * SW_BLOCK] + next_mask, s_t[3 * SW_BLOCK:]], axis=0)

    def sw_probs(i, s_t):
        return _sw_probs(s_t, sink)

    def sw_finish(i, probs):
        v_t = jnp.concatenate([swvt_ref[0, :, pl.ds(st, SW_BLOCK)] for st in sw_starts(i)] + [cswvt_ref[0]], axis=1)
        base = A_WIDTH + NA_WIDTH
        for j, slab in enumerate(_sw_finish(*probs, v_t)):
            y_ref[0, i * SW_BLOCK:(i + 1) * SW_BLOCK, base + j * LANES:base + (j + 1) * LANES] = slab.astype(BF16)

    def pipelined(n, stages):
        vals = {}
        for t in range(n + len(stages) - 1):
            for k, stage in reversed(list(enumerate(stages))):
                if 0 <= t - k < n:
                    vals[k, t - k] = stage(t - k, vals.pop((k - 1, t - k), None))

    for c in range(tq // CHUNK):
        sl = slice(c * CHUNK, (c + 1) * CHUNK)
        y_ref[0, sl, 0:A_WIDTH] = _gmlp_chunk(u_ref[0, sl, :], v_ref[0, sl, :], ws_all, bsm).astype(BF16)
    pipelined(rows_per_tile, [na_scores, na_probs, na_finish])
    pipelined(tq // SW_BLOCK, [sw_scores, sw_probs, sw_finish])


def _mixer(u, v, naq, nak, navt, swq, swk, swvt, cnak, cnavt, cswk, cswvt,
           ws_all, bsm, bias_tab, sw_mask, sink, *, tq):
    b, s, _ = u.shape
    d = A_WIDTH + NA_WIDTH + SW_Q_WIDTH
    lc = cnak.shape[1]
    tile = lambda w: pl.BlockSpec((1, tq, w), lambda i, j: (i, j, 0))
    full = lambda n, w: pl.BlockSpec((1, n, w), lambda i, j: (i, 0, 0))
    const = lambda shape: pl.BlockSpec(shape, lambda i, j: (0,) * len(shape))
    return pl.pallas_call(
        functools.partial(_mixer_kernel, tq=tq, seq=s),
        grid=(b, s // tq),
        in_specs=[tile(A_WIDTH), tile(A_WIDTH),
                  tile(NA_WIDTH), full(s, NA_WIDTH), full(NA_VT_ROWS, s),
                  tile(SW_Q_WIDTH), full(s, SW_KV_WIDTH), full(SW_VT_ROWS, s),
                  full(lc, NA_WIDTH), full(NA_VT_ROWS, lc), full(lc, SW_KV_WIDTH), full(SW_VT_ROWS, lc),
                  const(ws_all.shape), const(bsm.shape), const(bias_tab.shape), const(sw_mask.shape),
                  const(sink.shape)],
        out_specs=tile(d),
        out_shape=jax.ShapeDtypeStruct((b, s, d), BF16),
        scratch_shapes=[pltpu.VMEM((2, NA_VT_ROWS, s), BF16)],
        compiler_params=_params(2),
        name="mixer",
    )(u, v, naq, nak, navt, swq, swk, swvt, cnak, cnavt, cswk, cswvt,
      ws_all, bsm, bias_tab, sw_mask, sink)


def _ctx_mixer_kernel(u_ref, v_ref, naq_ref, nak_ref, navt_ref, swq_ref, swk_ref, swvt_ref,
                      ws_ref, bsm_ref, sink_ref, y_ref, *, lc):
    ws_all = ws_ref[...]
    bsm = bsm_ref[...]
    for c in range(lc // CHUNK):
        sl = slice(c * CHUNK, (c + 1) * CHUNK)
        y_ref[0, sl, 0:A_WIDTH] = _gmlp_chunk(u_ref[0, sl, :], v_ref[0, sl, :], ws_all, bsm).astype(BF16)

    out = _na_finish(_exp_t(_na_scores(naq_ref[0], nak_ref[0]))[0], navt_ref[0])
    y_ref[0, :, A_WIDTH:A_WIDTH + NA_WIDTH] = out.astype(BF16)

    slabs = _sw_finish(*_sw_probs(_sw_scores(swq_ref[0], swk_ref[0]), sink_ref[...]), swvt_ref[0])
    base = A_WIDTH + NA_WIDTH
    for j, slab in enumerate(slabs):
        y_ref[0, :, base + j * LANES:base + (j + 1) * LANES] = slab.astype(BF16)


def _ctx_mixer(u, v, naq, nak, navt, swq, swk, swvt, ws_all, bsm, sink):
    b, lc, _ = u.shape
    d = A_WIDTH + NA_WIDTH + SW_Q_WIDTH
    full = lambda n, w: pl.BlockSpec((1, n, w), lambda i: (i, 0, 0))
    const = lambda shape: pl.BlockSpec(shape, lambda i: (0,) * len(shape))
    return pl.pallas_call(
        functools.partial(_ctx_mixer_kernel, lc=lc),
        grid=(b,),
        in_specs=[full(lc, A_WIDTH), full(lc, A_WIDTH), full(lc, NA_WIDTH), full(lc, NA_WIDTH),
                  full(NA_VT_ROWS, lc), full(lc, SW_Q_WIDTH), full(lc, SW_KV_WIDTH), full(SW_VT_ROWS, lc),
                  const(ws_all.shape), const(bsm.shape), const(sink.shape)],
        out_specs=full(lc, d),
        out_shape=jax.ShapeDtypeStruct((b, lc, d), BF16),
        compiler_params=_params(1),
        name="ctx_mixer",
    )(u, v, naq, nak, navt, swq, swk, swvt, ws_all, bsm, sink)


def _out_ffn_kernel(y_ref, x_ref, mod_ref, wout_ref, l1g_ref, l1b_ref, w1_ref, w2_ref, l2g_ref, l2b_ref,
                    o_ref, *, alpha):
    tm = x_ref.shape[1]
    halves = [slice(0, tm // 2), slice(tm // 2, tm)]
    yo = [jnp.dot(y_ref[0, sl, :], wout_ref[...], preferred_element_type=F32) for sl in halves]
    x, a = [], []
    for sl, yo_h in zip(halves, yo):
        x_h = _layer_norm(alpha * x_ref[0, sl, :] + mod_ref[0, 2:3, :] * yo_h, l1g_ref[...], l1b_ref[...])
        h = (x_h * (1.0 + mod_ref[0, 4:5, :]) + mod_ref[0, 3:4, :]).astype(BF16)
        x.append(x_h)
        a.append(jnp.dot(h, w1_ref[...], preferred_element_type=F32))
    f = [jnp.dot(jnp.square(jnp.maximum(a_h, 0.0)).astype(BF16), w2_ref[...], preferred_element_type=F32)
         for a_h in a]
    for sl, x_h, f_h in zip(halves, x, f):
        o_ref[0, sl, :] = _layer_norm(alpha * x_h + mod_ref[0, 5:6, :] * f_h, l2g_ref[...], l2b_ref[...])


def _out_ffn(y, x, mod6, wout, l1g, l1b, w1, w2, l2g, l2b, *, tm, alpha):
    b, s, d = x.shape
    tile = pl.BlockSpec((1, tm, d), lambda i, j: (i, j, 0))
    const = lambda shape: pl.BlockSpec(shape, lambda i, j: (0,) * len(shape),
                                       pipeline_mode=pl.Buffered(1))
    return pl.pallas_call(
        functools.partial(_out_ffn_kernel, alpha=alpha),
        grid=(b, s // tm),
        in_specs=[tile, tile, pl.BlockSpec((1, 6, d), lambda i, j: (i, 0, 0)),
                  const(wout.shape), const((1, d)), const((1, d)),
                  const(w1.shape), const(w2.shape), const((1, d)), const((1, d))],
        out_specs=tile,
        out_shape=jax.ShapeDtypeStruct((b, s, d), F32),
        compiler_params=_params(2),
        name="out_ffn",
    )(y, x, mod6, wout, l1g, l1b, w1, w2, l2g, l2b)


def _sw_head_order():
    return [kv * SW_GROUP + j for j in range(SW_GROUP) for kv in range(SW_KV_HEADS)]


def _rope_tables(seq):
    pos = jnp.arange(seq)
    row_pos, col_pos = pos // GRID_W, pos % GRID_W
    inv_freq = ROPE_BASE ** (-jnp.arange(ROPE_FREQS, dtype=F32) / ROPE_FREQS)
    ang_r = row_pos.astype(F32)[:, None] * inv_freq
    ang_c = col_pos.astype(F32)[:, None] * inv_freq
    cos_head = jnp.concatenate([jnp.cos(ang_r), jnp.cos(ang_r), jnp.cos(ang_c), jnp.cos(ang_c)], axis=-1)
    sin_head = jnp.concatenate([-jnp.sin(ang_r), jnp.sin(ang_r), -jnp.sin(ang_c), jnp.sin(ang_c)], axis=-1)
    return jnp.tile(cos_head, (1, LANES // HEAD_DIM)), jnp.tile(sin_head, (1, LANES // HEAD_DIM))


def _na_bias_table(rpb):
    cq = np.arange(GRID_W)
    cs = np.clip(cq - NA_COLS // 2, 0, GRID_W - NA_COLS)
    col_ok = (cq[None, :] >= cs[:, None]) & (cq[None, :] < cs[:, None] + NA_COLS)
    dc = np.clip(cq[None, :] - cq[:, None], -(NA_COLS - 1), NA_COLS - 1) + NA_COLS - 1
    bias_col = jnp.where(col_ok[None, None], rpb[:, :, dc] * LOG2E, NEG_INF)
    return bias_col.transpose(1, 3, 0, 2).reshape(2 * NA_ROWS - 1, GRID_W, NA_HEADS * GRID_W)


def _sw_mask_table():
    kj = np.arange(SW_BLOCK)[:, None]
    qi = np.arange(SW_BLOCK)[None, :]
    prev = np.where(kj >= qi, 0.0, NEG_INF)
    nxt = np.where(kj <= qi, 0.0, NEG_INF)
    none = np.full((SW_BLOCK, SW_BLOCK), NEG_INF)
    return jnp.asarray(np.stack([np.tile(m, (1, SW_HEADS)) for m in (prev, nxt, none)]), F32)


def kernel(x, c, ctx, c_ctx, w_mod, b_mod, w_in, a_ln_g, a_ln_b, a_ws, a_bs, na_rpb, sw_sink, w_out,
           ln1_g, ln1_b, w1, w2, ln2_g, ln2_b):
    depth = w_mod.shape[0]
    b, s, d = x.shape
    lc = ctx.shape[1]
    alpha = (2 * depth) ** 0.25
    tq = 512

    rows = -(-(b + 1) // MOD_ROWS_PAD) * MOD_ROWS_PAD
    cc = jnp.concatenate([c, c_ctx[None], jnp.zeros((rows - b - 1, d), F32)], axis=0)
    mod_all = _modulation(cc, w_mod, b_mod)

    q0 = 2 * A_WIDTH + 3 * NA_WIDTH
    nav0 = 2 * A_WIDTH + 2 * NA_WIDTH
    k0 = q0 + SW_Q_WIDTH
    heads = _sw_head_order()
    cos, sin = _rope_tables(s)
    sw_mask = _sw_mask_table()

    xc = ctx
    for layer in range(depth):
        mod6 = mod_all[layer, :b].reshape(b, 6, d)
        modc6 = jnp.broadcast_to(mod_all[layer, b].reshape(1, 6, d), (b, 6, d))
        wl = w_in[layer]
        w_main = jnp.concatenate(
            [wl[:, :nav0]] + [wl[:, q0 + h * HEAD_DIM:q0 + (h + 1) * HEAD_DIM] for h in heads]
            + [wl[:, k0:k0 + SW_KV_WIDTH]], axis=1).astype(BF16)
        w_vt = jnp.concatenate([wl[:, nav0:q0], wl[:, k0 + SW_KV_WIDTH:]], axis=1).T.astype(BF16)
        wo = w_out[layer]
        y0 = A_WIDTH + NA_WIDTH
        w_out_l = jnp.concatenate(
            [wo[:y0]] + [wo[y0 + h * HEAD_DIM:y0 + (h + 1) * HEAD_DIM] for h in heads], axis=0).astype(BF16)
        w1_l = w1[layer].astype(BF16)
        w2_l = w2[layer].astype(BF16)
        lng = a_ln_g[layer].reshape(1, A_WIDTH)
        lnb = a_ln_b[layer].reshape(1, A_WIDTH)
        ws_all = a_ws[layer].reshape(A_GROUPS * CHUNK, CHUNK).astype(BF16)
        bsm = jnp.repeat(a_bs[layer].T, HEAD_DIM, axis=1)
        bias_tab = _na_bias_table(na_rpb[layer])
        sink8 = sw_sink[layer] * LOG2E
        sink_blk = jnp.repeat(sink8, SW_BLOCK)[None, :]
        sink_ctx = jnp.repeat(sink8, lc)[None, :]
        l1g, l1b = ln1_g[layer].reshape(1, d), ln1_b[layer].reshape(1, d)
        l2g, l2b = ln2_g[layer].reshape(1, d), ln2_b[layer].reshape(1, d)

        u, v, naq, nak, navt, swq, swk, swvt = _in_projection(
            x, mod6, w_main, w_vt, lng, lnb, cos, sin, rope=True, tm=1024)
        cu, cv, cnaq, cnak, cnavt, cswq, cswk, cswvt = _in_projection(
            xc, modc6, w_main, w_vt, lng, lnb, cos, sin, rope=False, tm=lc)

        y = _mixer(u, v, naq, nak, navt, swq, swk, swvt, cnak, cnavt, cswk, cswvt,
                   ws_all, bsm, bias_tab, sw_mask, sink_blk, tq=tq)
        x = _out_ffn(y, x, mod6, w_out_l, l1g, l1b, w1_l, w2_l, l2g, l2b, tm=512, alpha=alpha)

        if layer < depth - 1:
            yc = _ctx_mixer(cu, cv, cnaq, cnak, cnavt, cswq, cswk, cswvt, ws_all, bsm, sink_ctx)
            xc = _out_ffn(yc, xc, modc6, w_out_l, l1g, l1b, w1_l, w2_l, l2g, l2b, tm=lc, alpha=alpha)
    return x
```

```python
import functools
import math

import numpy as np
import jax
import jax.numpy as jnp
from jax import lax
from jax.experimental import pallas as pl
from jax.experimental.pallas import tpu as pltpu

F32 = jnp.float32
BF16 = jnp.bfloat16

D_MODEL = 1024
HEAD_DIM = 64
GRID_W = 64
CHUNK = 128
A_GROUPS = 4
A_WIDTH = 256
NA_HEADS = 4
NA_WIDTH = 256
NA_ROWS = 8
NA_COLS = 16
SW_KV_HEADS = 2
SW_GROUP = 4
SW_HEADS = SW_KV_HEADS * SW_GROUP
SW_Q_WIDTH = 512
SW_KV_WIDTH = 128
SW_BLOCK = 128
IN_WIDTH = 2048
VT_WIDTH = NA_WIDTH + SW_KV_WIDTH
ONES_ROWS = 16
NA_VT_ROWS = NA_WIDTH + ONES_ROWS
SW_VT_ROWS = SW_KV_WIDTH + ONES_ROWS
ROPE_BASE = 10000.0
ROPE_FREQS = 16
LN_EPS = 1e-5
NEG_INF = -1e30
LOG2E = math.log2(math.e)
Q_SCALE = HEAD_DIM ** -0.5 * LOG2E

LANES = 128
MOD_ROWS_PAD = 8
VMEM_LIMIT = 56 * 1024 * 1024


def _layer_norm(x, g, b):
    mu = jnp.mean(x, axis=-1, keepdims=True)
    var = jnp.mean(jnp.square(x - mu), axis=-1, keepdims=True)
    return (x - mu) * lax.rsqrt(var + LN_EPS) * g + b


def _params(n_grid):
    return pltpu.CompilerParams(dimension_semantics=("arbitrary",) * n_grid,
                                vmem_limit_bytes=VMEM_LIMIT)


def _nt_dot(a, b):
    return lax.dot_general(a, b, (((1,), (1,)), ((), ())), preferred_element_type=F32)


def _mod_kernel(c_ref, w_ref, b_ref, o_ref):
    a = jax.nn.silu(c_ref[...]).astype(BF16)
    o_ref[0] = jnp.dot(a, w_ref[0].astype(BF16), preferred_element_type=F32) + b_ref[0]


def _modulation(cc, w_mod, b_mod, tn=1024):
    depth, d, n = w_mod.shape
    rows = cc.shape[0]
    return pl.pallas_call(
        _mod_kernel,
        grid=(depth, n // tn),
        in_specs=[pl.BlockSpec((rows, d), lambda l, j: (0, 0)),
                  pl.BlockSpec((1, d, tn), lambda l, j: (l, 0, j)),
                  pl.BlockSpec((1, 1, tn), lambda l, j: (l, 0, j))],
        out_specs=pl.BlockSpec((1, rows, tn), lambda l, j: (l, 0, j)),
        out_shape=jax.ShapeDtypeStruct((depth, rows, n), F32),
        compiler_params=_params(2),
        name="modulation",
    )(cc, w_mod, b_mod.reshape(depth, 1, n))


def _inproj_kernel(x_ref, mod_ref, w_ref, wvt_ref, lng_ref, lnb_ref, cos_ref, sin_ref,
                   u_ref, v_ref, naq_ref, nak_ref, navt_ref, swq_ref, swk_ref, swvt_ref, *, rope):
    x = x_ref[0]
    h = (x * (1.0 + mod_ref[0, 1:2, :]) + mod_ref[0, 0:1, :]).astype(BF16)

    def proj(lo, hi):
        return jnp.dot(h, w_ref[:, lo:hi], preferred_element_type=F32)

    if rope:
        cos = cos_ref[...]
        sin = sin_ref[...]
        first = (lax.broadcasted_iota(jnp.int32, (1, LANES), 1) % 32) < 16

        def rot(t):
            partner = jnp.where(first, pltpu.roll(t, LANES - 16, axis=1), pltpu.roll(t, 16, axis=1))
            return t * cos + partner * sin
    else:
        def rot(t):
            return t

    u_ref[0] = jax.nn.gelu(proj(0, 256)).astype(BF16)
    v = jax.nn.gelu(proj(256, 512))
    v_ref[0] = _layer_norm(v, lng_ref[...], lnb_ref[...]).astype(BF16)
    naq_ref[0] = (proj(512, 768) * Q_SCALE).astype(BF16)
    nak_ref[0] = proj(768, 1024).astype(BF16)
    qk = proj(1024, 1664)
    for j in range(SW_Q_WIDTH // LANES):
        sl = slice(j * LANES, (j + 1) * LANES)
        swq_ref[0, :, sl] = (rot(qk[:, sl]) * Q_SCALE).astype(BF16)
    swk_ref[0] = rot(qk[:, SW_Q_WIDTH:]).astype(BF16)
    vt = _nt_dot(wvt_ref[...], h)
    ones = jnp.ones((ONES_ROWS, vt.shape[1]), BF16)
    navt_ref[0, :NA_WIDTH, :] = vt[:NA_WIDTH].astype(BF16)
    navt_ref[0, NA_WIDTH:, :] = ones
    swvt_ref[0, :SW_KV_WIDTH, :] = vt[NA_WIDTH:].astype(BF16)
    swvt_ref[0, SW_KV_WIDTH:, :] = ones


def _in_projection(x, mod6, w_main, w_vt, ln_g, ln_b, cos, sin, *, rope, tm):
    b, s, d = x.shape
    tile = lambda w: pl.BlockSpec((1, tm, w), lambda i, j: (i, j, 0))
    tile_t = lambda w: pl.BlockSpec((1, w, tm), lambda i, j: (i, 0, j))
    const = lambda shape: pl.BlockSpec(shape, lambda i, j: (0,) * len(shape))
    nat = lambda w: jax.ShapeDtypeStruct((b, s, w), BF16)
    tra = lambda w: jax.ShapeDtypeStruct((b, w, s), BF16)
    return pl.pallas_call(
        functools.partial(_inproj_kernel, rope=rope),
        grid=(b, s // tm),
        in_specs=[tile(d),
                  pl.BlockSpec((1, 6, d), lambda i, j: (i, 0, 0)),
                  const(w_main.shape), const(w_vt.shape),
                  const((1, A_WIDTH)), const((1, A_WIDTH)),
                  pl.BlockSpec((tm, LANES), lambda i, j: (j, 0)),
                  pl.BlockSpec((tm, LANES), lambda i, j: (j, 0))],
        out_specs=[tile(A_WIDTH), tile(A_WIDTH), tile(NA_WIDTH), tile(NA_WIDTH), tile_t(NA_VT_ROWS),
                   tile(SW_Q_WIDTH), tile(SW_KV_WIDTH), tile_t(SW_VT_ROWS)],
        out_shape=[nat(A_WIDTH), nat(A_WIDTH), nat(NA_WIDTH), nat(NA_WIDTH), tra(NA_VT_ROWS),
                   nat(SW_Q_WIDTH), nat(SW_KV_WIDTH), tra(SW_VT_ROWS)],
        compiler_params=_params(2),
        name="in_projection_rope" if rope else "in_projection",
    )(x, mod6, w_main, w_vt, ln_g, ln_b, cos, sin)


def _gmlp_chunk(u, v, ws_all, bsm):
    m = jnp.dot(ws_all, v, preferred_element_type=F32)
    group = lax.broadcasted_iota(jnp.int32, (1, A_WIDTH), 1) // HEAD_DIM
    mixed = m[0:CHUNK]
    for g in range(1, A_GROUPS):
        mixed = jnp.where(group == g, m[g * CHUNK:(g + 1) * CHUNK], mixed)
    return u.astype(F32) * (mixed + bsm)


def _stack_na_q(q):
    head = lax.broadcasted_iota(jnp.int32, (1, NA_WIDTH), 1) // HEAD_DIM
    zero = jnp.zeros_like(q)
    return jnp.concatenate([jnp.where(head == h, q, zero) for h in range(NA_HEADS)], axis=0)


def _stack_sw_q(q):
    low = lax.broadcasted_iota(jnp.int32, (1, LANES), 1) < HEAD_DIM
    slabs = [q[:, j * LANES:(j + 1) * LANES] for j in range(SW_GROUP)]
    zero = jnp.zeros_like(slabs[0])
    return jnp.concatenate([jnp.where(low, qj, zero) for qj in slabs]
                           + [jnp.where(low, zero, qj) for qj in slabs], axis=0)


def _exp_t(s_t, sink=None):
    m = jnp.max(s_t, axis=0, keepdims=True)
    if sink is not None:
        m = jnp.maximum(m, sink)
    return jnp.exp2((s_t - m).astype(BF16)), m


def _na_scores(q, kcat):
    return _nt_dot(kcat, _stack_na_q(q))


def _na_finish(e, v_t):
    n = e.shape[1] // NA_HEADS
    o_t = jnp.dot(v_t, e, preferred_element_type=F32)
    o = (o_t[:NA_WIDTH] * (1.0 / o_t[NA_WIDTH:NA_WIDTH + 1])).T
    head = lax.broadcasted_iota(jnp.int32, (1, NA_WIDTH), 1) // HEAD_DIM
    out = o[0:n]
    for h in range(1, NA_HEADS):
        out = jnp.where(head == h, o[h * n:(h + 1) * n], out)
    return out


def _sw_scores(q, kcat):
    return _nt_dot(kcat, _stack_sw_q(q))


def _sw_probs(s_t, sink):
    e, m = _exp_t(s_t, sink)
    return e, jnp.exp2(sink - m)


def _sw_finish(e, sink_e, v_t):
    n = e.shape[1] // SW_HEADS
    ones = v_t[SW_KV_WIDTH:]
    o_kv = []
    for kv in range(SW_KV_HEADS):
        cols = slice(kv * SW_GROUP * n, (kv + 1) * SW_GROUP * n)
        v_kv = jnp.concatenate([v_t[kv * HEAD_DIM:(kv + 1) * HEAD_DIM], ones], axis=0)
        o_t = jnp.dot(v_kv, e[:, cols], preferred_element_type=F32)
        inv_l = 1.0 / (o_t[HEAD_DIM:HEAD_DIM + 1] + sink_e[:, cols])
        o_kv.append(o_t[:HEAD_DIM] * inv_l)
    return [jnp.concatenate([o[:, j * n:(j + 1) * n] for o in o_kv], axis=0).T for j in range(SW_GROUP)]


def _mixer_kernel(u_ref, v_ref, naq_ref, nak_ref, navt_ref, swq_ref, swk_ref, swvt_ref,
                  cnak_ref, cnavt_ref, cswk_ref, cswvt_ref,
                  ws_ref, bsm_ref, bias_ref, swmask_ref, sink_ref,
                  y_ref, navt2_scr, *, tq, seq):
    tile = pl.program_id(1)
    rows_per_tile = tq // GRID_W
    n_rows = seq // GRID_W
    win_keys = NA_ROWS * GRID_W

    @pl.when(tile == 0)
    def _():
        navt2_scr[0] = navt_ref[0]
        navt2_scr[1, :, 0:seq - GRID_W] = navt_ref[0, :, GRID_W:seq]
        navt2_scr[1, :, seq - GRID_W:seq] = jnp.zeros((NA_VT_ROWS, GRID_W), BF16)

    ws_all = ws_ref[...]
    bsm = bsm_ref[...]

    def na_window(i):
        r = tile * rows_per_tile + i
        rs = jnp.clip(r - NA_ROWS // 2, 0, n_rows - NA_ROWS)
        return rs, r - rs

    def na_scores(i, _):
        rs, off = na_window(i)
        kstart = pl.multiple_of(rs * GRID_W, GRID_W)
        kcat = jnp.concatenate([nak_ref[0, pl.ds(kstart, win_keys), :], cnak_ref[0]], axis=0)
        s_t = _na_scores(naq_ref[0, i * GRID_W:(i + 1) * GRID_W, :], kcat)
        parts = [s_t[j * GRID_W:(j + 1) * GRID_W] + bias_ref[j + NA_ROWS - 1 - off] for j in range(NA_ROWS)]
        return jnp.concatenate(parts + [s_t[win_keys:]], axis=0)

    def na_probs(i, s_t):
        return _exp_t(s_t)[0]

    def na_finish(i, e):
        rs, _ = na_window(i)
        par = rs % 2
        vstart = pl.multiple_of((rs - par) * GRID_W, LANES)
        v_t = jnp.concatenate([navt2_scr[par, :, pl.ds(vstart, win_keys)], cnavt_ref[0]], axis=1)
        y_ref[0, i * GRID_W:(i + 1) * GRID_W, A_WIDTH:A_WIDTH + NA_WIDTH] = _na_finish(e, v_t).astype(BF16)

    n_blocks = seq // SW_BLOCK
    sink = sink_ref[...]

    def sw_starts(i):
        t0 = (tile * (tq // SW_BLOCK) + i) * SW_BLOCK
        return (pl.multiple_of(jnp.maximum(t0 - SW_BLOCK, 0), SW_BLOCK), pl.multiple_of(t0, SW_BLOCK),
                pl.multiple_of(jnp.minimum(t0 + SW_BLOCK, seq - SW_BLOCK), SW_BLOCK))

    def sw_scores(i, _):
        bi = tile * (tq // SW_BLOCK) + i
        kcat = jnp.concatenate([swk_ref[0, pl.ds(st, SW_BLOCK), :] for st in sw_starts(i)] + [cswk_ref[0]], axis=0)
        s_t = _sw_scores(swq_ref[0, i * SW_BLOCK:(i + 1) * SW_BLOCK, :], kcat)
        prev_mask = swmask_ref[jnp.where(bi > 0, 0, 2)]
        next_mask = swmask_ref[jnp.where(bi < n_blocks - 1, 1, 2)]
        return jnp.concatenate([s_t[0:SW_BLOCK] + prev_mask, s_t[SW_BLOCK:2 * SW_BLOCK],
                                s_t[2 * SW_BLOCK:3 * SW_BLOCK] + next_mask, s_t[3 * SW_BLOCK:]], axis=0)

    def sw_probs(i, s_t):
        return _sw_probs(s_t, sink)

    def sw_finish(i, probs):
        v_t = jnp.concatenate([swvt_ref[0, :, pl.ds(st, SW_BLOCK)] for st in sw_starts(i)] + [cswvt_ref[0]], axis=1)
        base = A_WIDTH + NA_WIDTH
        for j, slab in enumerate(_sw_finish(*probs, v_t)):
            y_ref[0, i * SW_BLOCK:(i + 1) * SW_BLOCK, base + j * LANES:base + (j + 1) * LANES] = slab.astype(BF16)

    def pipelined(n, stages):
        vals = {}
        for t in range(n + len(stages) - 1):
            for k, stage in reversed(list(enumerate(stages))):
                if 0 <= t - k < n:
                    vals[k, t - k] = stage(t - k, vals.pop((k - 1, t - k), None))

    for c in range(tq // CHUNK):
        sl = slice(c * CHUNK, (c + 1) * CHUNK)
        y_ref[0, sl, 0:A_WIDTH] = _gmlp_chunk(u_ref[0, sl, :], v_ref[0, sl, :], ws_all, bsm).astype(BF16)
    fns = {"na": [na_scores, na_probs, na_finish], "sw": [sw_scores, sw_probs, sw_finish]}
    items = []
    for c in range(tq // SW_BLOCK):
        items += [("na", 2 * c), ("na", 2 * c + 1), ("sw", c)]
    pipelined(len(items), [lambda t, v, k=k: fns[items[t][0]][k](items[t][1], v) for k in range(3)])


def _mixer(u, v, naq, nak, navt, swq, swk, swvt, cnak, cnavt, cswk, cswvt,
           ws_all, bsm, bias_tab, sw_mask, sink, *, tq):
    b, s, _ = u.shape
    d = A_WIDTH + NA_WIDTH + SW_Q_WIDTH
    lc = cnak.shape[1]
    tile = lambda w: pl.BlockSpec((1, tq, w), lambda i, j: (i, j, 0))
    full = lambda n, w: pl.BlockSpec((1, n, w), lambda i, j: (i, 0, 0))
    const = lambda shape: pl.BlockSpec(shape, lambda i, j: (0,) * len(shape))
    return pl.pallas_call(
        functools.partial(_mixer_kernel, tq=tq, seq=s),
        grid=(b, s // tq),
        in_specs=[tile(A_WIDTH), tile(A_WIDTH),
                  tile(NA_WIDTH), full(s, NA_WIDTH), full(NA_VT_ROWS, s),
                  tile(SW_Q_WIDTH), full(s, SW_KV_WIDTH), full(SW_VT_ROWS, s),
                  full(lc, NA_WIDTH), full(NA_VT_ROWS, lc), full(lc, SW_KV_WIDTH), full(SW_VT_ROWS, lc),
                  const(ws_all.shape), const(bsm.shape), const(bias_tab.shape), const(sw_mask.shape),
                  const(sink.shape)],
        out_specs=tile(d),
        out_shape=jax.ShapeDtypeStruct((b, s, d), BF16),
        scratch_shapes=[pltpu.VMEM((2, NA_VT_ROWS, s), BF16)],
        compiler_params=_params(2),
        name="mixer",
    )(u, v, naq, nak, navt, swq, swk, swvt, cnak, cnavt, cswk, cswvt,
      ws_all, bsm, bias_tab, sw_mask, sink)


def _ctx_mixer_kernel(u_ref, v_ref, naq_ref, nak_ref, navt_ref, swq_ref, swk_ref, swvt_ref,
                      ws_ref, bsm_ref, sink_ref, y_ref, *, lc):
    ws_all = ws_ref[...]
    bsm = bsm_ref[...]
    for c in range(lc // CHUNK):
        sl = slice(c * CHUNK, (c + 1) * CHUNK)
        y_ref[0, sl, 0:A_WIDTH] = _gmlp_chunk(u_ref[0, sl, :], v_ref[0, sl, :], ws_all, bsm).astype(BF16)

    out = _na_finish(_exp_t(_na_scores(naq_ref[0], nak_ref[0]))[0], navt_ref[0])
    y_ref[0, :, A_WIDTH:A_WIDTH + NA_WIDTH] = out.astype(BF16)

    slabs = _sw_finish(*_sw_probs(_sw_scores(swq_ref[0], swk_ref[0]), sink_ref[...]), swvt_ref[0])
    base = A_WIDTH + NA_WIDTH
    for j, slab in enumerate(slabs):
        y_ref[0, :, base + j * LANES:base + (j + 1) * LANES] = slab.astype(BF16)


def _ctx_mixer(u, v, naq, nak, navt, swq, swk, swvt, ws_all, bsm, sink):
    b, lc, _ = u.shape
    d = A_WIDTH + NA_WIDTH + SW_Q_WIDTH
    full = lambda n, w: pl.BlockSpec((1, n, w), lambda i: (i, 0, 0))
    const = lambda shape: pl.BlockSpec(shape, lambda i: (0,) * len(shape))
    return pl.pallas_call(
        functools.partial(_ctx_mixer_kernel, lc=lc),
        grid=(b,),
        in_specs=[full(lc, A_WIDTH), full(lc, A_WIDTH), full(lc, NA_WIDTH), full(lc, NA_WIDTH),
                  full(NA_VT_ROWS, lc), full(lc, SW_Q_WIDTH), full(lc, SW_KV_WIDTH), full(SW_VT_ROWS, lc),
                  const(ws_all.shape), const(bsm.shape), const(sink.shape)],
        out_specs=full(lc, d),
        out_shape=jax.ShapeDtypeStruct((b, lc, d), BF16),
        compiler_params=_params(1),
        name="ctx_mixer",
    )(u, v, naq, nak, navt, swq, swk, swvt, ws_all, bsm, sink)


def _out_ffn_kernel(y_ref, x_ref, mod_ref, wout_ref, l1g_ref, l1b_ref, w1_ref, w2_ref, l2g_ref, l2b_ref,
                    o_ref, *, alpha):
    tm = x_ref.shape[1]
    halves = [slice(0, tm // 2), slice(tm // 2, tm)]
    def out_proj(sl):
        return jnp.dot(y_ref[0, sl, :], wout_ref[...], preferred_element_type=F32)

    def norm1(sl, yo_h):
        x_h = _layer_norm(alpha * x_ref[0, sl, :] + mod_ref[0, 2:3, :] * yo_h, l1g_ref[...], l1b_ref[...])
        return x_h, (x_h * (1.0 + mod_ref[0, 4:5, :]) + mod_ref[0, 3:4, :]).astype(BF16)

    x0, h0 = norm1(halves[0], out_proj(halves[0]))
    yo1 = out_proj(halves[1])
    a0 = jnp.dot(h0, w1_ref[...], preferred_element_type=F32)
    x1, h1 = norm1(halves[1], yo1)
    a1 = jnp.dot(h1, w1_ref[...], preferred_element_type=F32)
    x, a = [x0, x1], [a0, a1]
    f = [jnp.dot(jnp.square(jnp.maximum(a_h, 0.0)).astype(BF16), w2_ref[...], preferred_element_type=F32)
         for a_h in a]
    for sl, x_h, f_h in zip(halves, x, f):
        o_ref[0, sl, :] = _layer_norm(alpha * x_h + mod_ref[0, 5:6, :] * f_h, l2g_ref[...], l2b_ref[...])


def _out_ffn(y, x, mod6, wout, l1g, l1b, w1, w2, l2g, l2b, *, tm, alpha):
    b, s, d = x.shape
    tile = pl.BlockSpec((1, tm, d), lambda i, j: (i, j, 0))
    const = lambda shape: pl.BlockSpec(shape, lambda i, j: (0,) * len(shape),
                                       pipeline_mode=pl.Buffered(1))
    return pl.pallas_call(
        functools.partial(_out_ffn_kernel, alpha=alpha),
        grid=(b, s // tm),
        in_specs=[tile, tile, pl.BlockSpec((1, 6, d), lambda i, j: (i, 0, 0)),
                  const(wout.shape), const((1, d)), const((1, d)),
                  const(w1.shape), const(w2.shape), const((1, d)), const((1, d))],
        out_specs=tile,
        out_shape=jax.ShapeDtypeStruct((b, s, d), F32),
        compiler_params=_params(2),
        name="out_ffn",
    )(y, x, mod6, wout, l1g, l1b, w1, w2, l2g, l2b)


def _sw_head_order():
    return [kv * SW_GROUP + j for j in range(SW_GROUP) for kv in range(SW_KV_HEADS)]


def _rope_tables(seq):
    pos = jnp.arange(seq)
    row_pos, col_pos = pos // GRID_W, pos % GRID_W
    inv_freq = ROPE_BASE ** (-jnp.arange(ROPE_FREQS, dtype=F32) / ROPE_FREQS)
    ang_r = row_pos.astype(F32)[:, None] * inv_freq
    ang_c = col_pos.astype(F32)[:, None] * inv_freq
    cos_head = jnp.concatenate([jnp.cos(ang_r), jnp.cos(ang_r), jnp.cos(ang_c), jnp.cos(ang_c)], axis=-1)
    sin_head = jnp.concatenate([-jnp.sin(ang_r), jnp.sin(ang_r), -jnp.sin(ang_c), jnp.sin(ang_c)], axis=-1)
    return jnp.tile(cos_head, (1, LANES // HEAD_DIM)), jnp.tile(sin_head, (1, LANES // HEAD_DIM))


def _na_bias_table(rpb):
    cq = np.arange(GRID_W)
    cs = np.clip(cq - NA_COLS // 2, 0, GRID_W - NA_COLS)
    col_ok = (cq[None, :] >= cs[:, None]) & (cq[None, :] < cs[:, None] + NA_COLS)
    ext = GRID_W - NA_COLS
    rpb_ext = jnp.concatenate([jnp.repeat(rpb[..., :1], ext, axis=-1), rpb,
                               jnp.repeat(rpb[..., -1:], ext, axis=-1)], axis=-1)
    rel = jnp.stack([rpb_ext[..., GRID_W - 1 - q:2 * GRID_W - 1 - q] for q in range(GRID_W)], axis=2)
    bias_col = jnp.where(col_ok[None, None], rel * LOG2E, NEG_INF)
    return bias_col.transpose(1, 3, 0, 2).reshape(2 * NA_ROWS - 1, GRID_W, NA_HEADS * GRID_W)


def _sw_mask_table():
    kj = np.arange(SW_BLOCK)[:, None]
    qi = np.arange(SW_BLOCK)[None, :]
    prev = np.where(kj >= qi, 0.0, NEG_INF)
    nxt = np.where(kj <= qi, 0.0, NEG_INF)
    none = np.full((SW_BLOCK, SW_BLOCK), NEG_INF)
    return jnp.asarray(np.stack([np.tile(m, (1, SW_HEADS)) for m in (prev, nxt, none)]), F32)


def kernel(x, c, ctx, c_ctx, w_mod, b_mod, w_in, a_ln_g, a_ln_b, a_ws, a_bs, na_rpb, sw_sink, w_out,
           ln1_g, ln1_b, w1, w2, ln2_g, ln2_b):
    depth = w_mod.shape[0]
    b, s, d = x.shape
    lc = ctx.shape[1]
    alpha = (2 * depth) ** 0.25
    tq = 512
    assert s % (2 * tq) == 0 and (b * lc) % tq == 0 and lc % CHUNK == 0

    rows = -(-(b + 1) // MOD_ROWS_PAD) * MOD_ROWS_PAD
    cc = jnp.concatenate([c, c_ctx[None], jnp.zeros((rows - b - 1, d), F32)], axis=0)
    mod_all = _modulation(cc, w_mod, b_mod)

    q0 = 2 * A_WIDTH + 3 * NA_WIDTH
    nav0 = 2 * A_WIDTH + 2 * NA_WIDTH
    k0 = q0 + SW_Q_WIDTH
    heads = _sw_head_order()
    cos, sin = _rope_tables(s)
    sw_mask = _sw_mask_table()

    xc = ctx
    for layer in range(depth):
        mod6 = mod_all[layer, :b].reshape(b, 6, d)
        modc6 = jnp.broadcast_to(mod_all[layer, b].reshape(1, 6, d), (b, 6, d))
        wl = w_in[layer]
        w_main = jnp.concatenate(
            [wl[:, :nav0]] + [wl[:, q0 + h * HEAD_DIM:q0 + (h + 1) * HEAD_DIM] for h in heads]
            + [wl[:, k0:k0 + SW_KV_WIDTH]], axis=1).astype(BF16)
        w_vt = jnp.concatenate([wl[:, nav0:q0], wl[:, k0 + SW_KV_WIDTH:]], axis=1).T.astype(BF16)
        wo = w_out[layer]
        y0 = A_WIDTH + NA_WIDTH
        w_out_l = jnp.concatenate(
            [wo[:y0]] + [wo[y0 + h * HEAD_DIM:y0 + (h + 1) * HEAD_DIM] for h in heads], axis=0).astype(BF16)
        w1_l = w1[layer].astype(BF16)
        w2_l = w2[layer].astype(BF16)
        lng = a_ln_g[layer].reshape(1, A_WIDTH)
        lnb = a_ln_b[layer].reshape(1, A_WIDTH)
        ws_all = a_ws[layer].reshape(A_GROUPS * CHUNK, CHUNK).astype(BF16)
        bsm = jnp.repeat(a_bs[layer].T, HEAD_DIM, axis=1)
        bias_tab = _na_bias_table(na_rpb[layer])
        sink8 = sw_sink[layer] * LOG2E
        sink_blk = jnp.repeat(sink8, SW_BLOCK)[None, :]
        sink_ctx = jnp.repeat(sink8, lc)[None, :]
        l1g, l1b = ln1_g[layer].reshape(1, d), ln1_b[layer].reshape(1, d)
        l2g, l2b = ln2_g[layer].reshape(1, d), ln2_b[layer].reshape(1, d)

        u, v, naq, nak, navt, swq, swk, swvt = _in_projection(
            x, mod6, w_main, w_vt, lng, lnb, cos, sin, rope=True, tm=1024)
        cu, cv, cnaq, cnak, cnavt, cswq, cswk, cswvt = _in_projection(
            xc, modc6, w_main, w_vt, lng, lnb, cos, sin, rope=False, tm=lc)

        y = _mixer(u, v, naq, nak, navt, swq, swk, swvt, cnak, cnavt, cswk, cswvt,
                   ws_all, bsm, bias_tab, sw_mask, sink_blk, tq=tq)
        x = _out_ffn(y, x, mod6, w_out_l, l1g, l1b, w1_l, w2_l, l2g, l2b, tm=512, alpha=alpha)

        if layer < depth - 1:
            yc = _ctx_mixer(cu, cv, cnaq, cnak, cnavt, cswq, cswk, cswvt, ws_all, bsm, sink_ctx)
            groups = b * lc // tq
            xc = _out_ffn(yc.reshape(groups, tq, d), xc.reshape(groups, tq, d), modc6[:groups], w_out_l, l1g, l1b,
                          w1_l, w2_l, l2g, l2b, tm=tq, alpha=alpha).reshape(b, lc, d)
    return x
```

```python
import functools
import math

import numpy as np
import jax
import jax.numpy as jnp
from jax import lax
from jax.experimental import pallas as pl
from jax.experimental.pallas import tpu as pltpu

F32 = jnp.float32
BF16 = jnp.bfloat16

D_MODEL = 1024
HEAD_DIM = 64
GRID_W = 64
CHUNK = 128
A_GROUPS = 4
A_WIDTH = 256
NA_HEADS = 4
NA_WIDTH = 256
NA_ROWS = 8
NA_COLS = 16
SW_KV_HEADS = 2
SW_GROUP = 4
SW_HEADS = SW_KV_HEADS * SW_GROUP
SW_Q_WIDTH = 512
SW_KV_WIDTH = 128
SW_BLOCK = 128
IN_WIDTH = 2048
VT_WIDTH = NA_WIDTH + SW_KV_WIDTH
ONES_ROWS = 16
NA_VT_ROWS = NA_WIDTH + ONES_ROWS
SW_VT_ROWS = SW_KV_WIDTH + ONES_ROWS
ROPE_BASE = 10000.0
ROPE_FREQS = 16
LN_EPS = 1e-5
NEG_INF = -1e30
LOG2E = math.log2(math.e)
Q_SCALE = HEAD_DIM ** -0.5 * LOG2E

LANES = 128
MOD_ROWS_PAD = 8
VMEM_LIMIT = 56 * 1024 * 1024


def _layer_norm(x, g, b):
    mu = jnp.mean(x, axis=-1, keepdims=True)
    var = jnp.mean(jnp.square(x - mu), axis=-1, keepdims=True)
    return (x - mu) * lax.rsqrt(var + LN_EPS) * g + b


def _params(n_grid):
    return pltpu.CompilerParams(dimension_semantics=("arbitrary",) * n_grid,
                                vmem_limit_bytes=VMEM_LIMIT)


def _nt_dot(a, b):
    return lax.dot_general(a, b, (((1,), (1,)), ((), ())), preferred_element_type=F32)


def _mod_kernel(c_ref, w_ref, b_ref, o_ref):
    a = jax.nn.silu(c_ref[...]).astype(BF16)
    o_ref[0] = jnp.dot(a, w_ref[0].astype(BF16), preferred_element_type=F32) + b_ref[0]


def _modulation(cc, w_mod, b_mod, tn=1024):
    depth, d, n = w_mod.shape
    rows = cc.shape[0]
    return pl.pallas_call(
        _mod_kernel,
        grid=(depth, n // tn),
        in_specs=[pl.BlockSpec((rows, d), lambda l, j: (0, 0)),
                  pl.BlockSpec((1, d, tn), lambda l, j: (l, 0, j)),
                  pl.BlockSpec((1, 1, tn), lambda l, j: (l, 0, j))],
        out_specs=pl.BlockSpec((1, rows, tn), lambda l, j: (l, 0, j)),
        out_shape=jax.ShapeDtypeStruct((depth, rows, n), F32),
        compiler_params=_params(2),
        name="modulation",
    )(cc, w_mod, b_mod.reshape(depth, 1, n))


def _inproj_kernel(x_ref, mod_ref, w_ref, wvt_ref, lng_ref, lnb_ref, cos_ref, sin_ref,
                   *out_refs, rope, kv_only):
    if kv_only:
        nak_ref, navt_ref, swk_ref, swvt_ref = out_refs
    else:
        u_ref, v_ref, naq_ref, nak_ref, navt_ref, swq_ref, swk_ref, swvt_ref = out_refs
    nb, tm, d = x_ref.shape
    x = x_ref[...].reshape(nb * tm, d)
    h = (x * (1.0 + mod_ref[0, 1:2, :]) + mod_ref[0, 0:1, :]).astype(BF16)

    def put(ref, val):
        ref[...] = val.astype(BF16).reshape(ref.shape)

    def proj(lo, hi):
        return jnp.dot(h, w_ref[:, lo:hi], preferred_element_type=F32)

    if rope:
        cos = cos_ref[...]
        sin = sin_ref[...]
        first = (lax.broadcasted_iota(jnp.int32, (1, LANES), 1) % 32) < 16

        def rot(t):
            partner = jnp.where(first, pltpu.roll(t, LANES - 16, axis=1), pltpu.roll(t, 16, axis=1))
            return t * cos + partner * sin
    else:
        def rot(t):
            return t

    if not kv_only:
        put(u_ref, jax.nn.gelu(proj(0, 256)))
        put(v_ref, _layer_norm(jax.nn.gelu(proj(256, 512)), lng_ref[...], lnb_ref[...]))
        put(naq_ref, proj(512, 768) * Q_SCALE)
    put(nak_ref, proj(768, 1024))
    if kv_only:
        put(swk_ref, rot(proj(1536, 1664)))
    else:
        qk = proj(1024, 1664)
        for j in range(SW_Q_WIDTH // LANES):
            sl = slice(j * LANES, (j + 1) * LANES)
            put(swq_ref.at[:, :, sl], rot(qk[:, sl]) * Q_SCALE)
        put(swk_ref, rot(qk[:, SW_Q_WIDTH:]))
    vt = _nt_dot(wvt_ref[...], h).astype(BF16)
    ones = jnp.ones((ONES_ROWS, tm), BF16)
    for i in range(nb):
        cols = slice(i * tm, (i + 1) * tm)
        navt_ref[i, :NA_WIDTH, :] = vt[:NA_WIDTH, cols]
        navt_ref[i, NA_WIDTH:, :] = ones
        swvt_ref[i, :SW_KV_WIDTH, :] = vt[NA_WIDTH:, cols]
        swvt_ref[i, SW_KV_WIDTH:, :] = ones


def _in_projection(x, mod6, w_main, w_vt, ln_g, ln_b, cos, sin, *, rope, tm, nb=1, kv_only=False):
    b, s, d = x.shape
    assert b % nb == 0 and not (rope and nb > 1)
    tile = lambda w: pl.BlockSpec((nb, tm, w), lambda i, j: (i, j, 0))
    tile_t = lambda w: pl.BlockSpec((nb, w, tm), lambda i, j: (i, 0, j))
    const = lambda shape: pl.BlockSpec(shape, lambda i, j: (0,) * len(shape))
    nat = lambda w: jax.ShapeDtypeStruct((b, s, w), BF16)
    tra = lambda w: jax.ShapeDtypeStruct((b, w, s), BF16)
    outs = [(tile(A_WIDTH), nat(A_WIDTH)), (tile(A_WIDTH), nat(A_WIDTH)), (tile(NA_WIDTH), nat(NA_WIDTH)),
            (tile(NA_WIDTH), nat(NA_WIDTH)), (tile_t(NA_VT_ROWS), tra(NA_VT_ROWS)),
            (tile(SW_Q_WIDTH), nat(SW_Q_WIDTH)), (tile(SW_KV_WIDTH), nat(SW_KV_WIDTH)),
            (tile_t(SW_VT_ROWS), tra(SW_VT_ROWS))]
    if kv_only:
        outs = [outs[3], outs[4], outs[6], outs[7]]
    return pl.pallas_call(
        functools.partial(_inproj_kernel, rope=rope, kv_only=kv_only),
        grid=(b // nb, s // tm),
        in_specs=[tile(d),
                  pl.BlockSpec((1, 6, d), lambda i, j: (i * nb, 0, 0)),
                  const(w_main.shape), const(w_vt.shape),
                  const((1, A_WIDTH)), const((1, A_WIDTH)),
                  pl.BlockSpec((tm, LANES), lambda i, j: (j, 0)),
                  pl.BlockSpec((tm, LANES), lambda i, j: (j, 0))],
        out_specs=[spec for spec, _ in outs],
        out_shape=[shape for _, shape in outs],
        compiler_params=_params(2),
        name="in_projection_rope" if rope else "in_projection",
    )(x, mod6, w_main, w_vt, ln_g, ln_b, cos, sin)


def _gmlp_chunk(u, v, ws_all, bsm):
    m = jnp.dot(ws_all, v, preferred_element_type=F32)
    group = lax.broadcasted_iota(jnp.int32, (1, A_WIDTH), 1) // HEAD_DIM
    mixed = m[0:CHUNK]
    for g in range(1, A_GROUPS):
        mixed = jnp.where(group == g, m[g * CHUNK:(g + 1) * CHUNK], mixed)
    return u.astype(F32) * (mixed + bsm)


def _stack_na_q(q):
    head = lax.broadcasted_iota(jnp.int32, (1, NA_WIDTH), 1) // HEAD_DIM
    zero = jnp.zeros_like(q)
    return jnp.concatenate([jnp.where(head == h, q, zero) for h in range(NA_HEADS)], axis=0)


def _stack_sw_q(q):
    low = lax.broadcasted_iota(jnp.int32, (1, LANES), 1) < HEAD_DIM
    slabs = [q[:, j * LANES:(j + 1) * LANES] for j in range(SW_GROUP)]
    zero = jnp.zeros_like(slabs[0])
    return jnp.concatenate([jnp.where(low, qj, zero) for qj in slabs]
                           + [jnp.where(low, zero, qj) for qj in slabs], axis=0)


def _exp_t(s_t, sink=None):
    m = jnp.max(s_t, axis=0, keepdims=True)
    if sink is not None:
        m = jnp.maximum(m, sink)
    return jnp.exp2((s_t - m).astype(BF16)), m


def _na_scores(q, kcat):
    return _nt_dot(kcat, _stack_na_q(q))


def _na_finish(e, v_t):
    n = e.shape[1] // NA_HEADS
    o_t = jnp.dot(v_t, e, preferred_element_type=F32)
    o = (o_t[:NA_WIDTH] * (1.0 / o_t[NA_WIDTH:NA_WIDTH + 1])).T
    head = lax.broadcasted_iota(jnp.int32, (1, NA_WIDTH), 1) // HEAD_DIM
    out = o[0:n]
    for h in range(1, NA_HEADS):
        out = jnp.where(head == h, o[h * n:(h + 1) * n], out)
    return out


def _sw_scores(q, kcat):
    return _nt_dot(kcat, _stack_sw_q(q))


def _sw_probs(s_t, sink):
    e, m = _exp_t(s_t, sink)
    return e, jnp.exp2(sink - m)


def _sw_finish(e, sink_e, v_t):
    n = e.shape[1] // SW_HEADS
    ones = v_t[SW_KV_WIDTH:]
    o_kv = []
    for kv in range(SW_KV_HEADS):
        cols = slice(kv * SW_GROUP * n, (kv + 1) * SW_GROUP * n)
        v_kv = jnp.concatenate([v_t[kv * HEAD_DIM:(kv + 1) * HEAD_DIM], ones], axis=0)
        o_t = jnp.dot(v_kv, e[:, cols], preferred_element_type=F32)
        inv_l = 1.0 / (o_t[HEAD_DIM:HEAD_DIM + 1] + sink_e[:, cols])
        o_kv.append(o_t[:HEAD_DIM] * inv_l)
    return [jnp.concatenate([o[:, j * n:(j + 1) * n] for o in o_kv], axis=0).T for j in range(SW_GROUP)]


def _mixer_kernel(u_ref, v_ref, naq_ref, nak_ref, navt_ref, swq_ref, swk_ref, swvt_ref,
                  cnak_ref, cnavt_ref, cswk_ref, cswvt_ref,
                  ws_ref, bsm_ref, bias_ref, swmask_ref, sink_ref,
                  y_ref, navt2_scr, *, tq, seq):
    tile = pl.program_id(1)
    rows_per_tile = tq // GRID_W
    n_rows = seq // GRID_W
    win_keys = NA_ROWS * GRID_W

    @pl.when(tile == 0)
    def _():
        navt2_scr[0] = navt_ref[0]
        navt2_scr[1, :, 0:seq - GRID_W] = navt_ref[0, :, GRID_W:seq]
        navt2_scr[1, :, seq - GRID_W:seq] = jnp.zeros((NA_VT_ROWS, GRID_W), BF16)

    ws_all = ws_ref[...]
    bsm = bsm_ref[...]

    def na_window(i):
        r = tile * rows_per_tile + i
        rs = jnp.clip(r - NA_ROWS // 2, 0, n_rows - NA_ROWS)
        return rs, r - rs

    def na_scores(i, _):
        rs, off = na_window(i)
        kstart = pl.multiple_of(rs * GRID_W, GRID_W)
        kcat = jnp.concatenate([nak_ref[0, pl.ds(kstart, win_keys), :], cnak_ref[0]], axis=0)
        s_t = _na_scores(naq_ref[0, i * GRID_W:(i + 1) * GRID_W, :], kcat)
        parts = [s_t[j * GRID_W:(j + 1) * GRID_W] + bias_ref[j + NA_ROWS - 1 - off] for j in range(NA_ROWS)]
        return jnp.concatenate(parts + [s_t[win_keys:]], axis=0)

    def na_probs(i, s_t):
        return _exp_t(s_t)[0]

    def na_finish(i, e):
        rs, _ = na_window(i)
        par = rs % 2
        vstart = pl.multiple_of((rs - par) * GRID_W, LANES)
        v_t = jnp.concatenate([navt2_scr[par, :, pl.ds(vstart, win_keys)], cnavt_ref[0]], axis=1)
        y_ref[0, i * GRID_W:(i + 1) * GRID_W, A_WIDTH:A_WIDTH + NA_WIDTH] = _na_finish(e, v_t).astype(BF16)

    n_blocks = seq // SW_BLOCK
    sink = sink_ref[...]

    def sw_starts(i):
        t0 = (tile * (tq // SW_BLOCK) + i) * SW_BLOCK
        return (pl.multiple_of(jnp.maximum(t0 - SW_BLOCK, 0), SW_BLOCK), pl.multiple_of(t0, SW_BLOCK),
                pl.multiple_of(jnp.minimum(t0 + SW_BLOCK, seq - SW_BLOCK), SW_BLOCK))

    def sw_scores(i, _):
        bi = tile * (tq // SW_BLOCK) + i
        kcat = jnp.concatenate([swk_ref[0, pl.ds(st, SW_BLOCK), :] for st in sw_starts(i)] + [cswk_ref[0]], axis=0)
        s_t = _sw_scores(swq_ref[0, i * SW_BLOCK:(i + 1) * SW_BLOCK, :], kcat)
        prev_mask = swmask_ref[jnp.where(bi > 0, 0, 2)]
        next_mask = swmask_ref[jnp.where(bi < n_blocks - 1, 1, 2)]
        return jnp.concatenate([s_t[0:SW_BLOCK] + prev_mask, s_t[SW_BLOCK:2 * SW_BLOCK],
                                s_t[2 * SW_BLOCK:3 * SW_BLOCK] + next_mask, s_t[3 * SW_BLOCK:]], axis=0)

    def sw_probs(i, s_t):
        return _sw_probs(s_t, sink)

    def sw_finish(i, probs):
        v_t = jnp.concatenate([swvt_ref[0, :, pl.ds(st, SW_BLOCK)] for st in sw_starts(i)] + [cswvt_ref[0]], axis=1)
        base = A_WIDTH + NA_WIDTH
        for j, slab in enumerate(_sw_finish(*probs, v_t)):
            y_ref[0, i * SW_BLOCK:(i + 1) * SW_BLOCK, base + j * LANES:base + (j + 1) * LANES] = slab.astype(BF16)

    def pipelined(n, stages):
        vals = {}
        for t in range(n + len(stages) - 1):
            for k, stage in reversed(list(enumerate(stages))):
                if 0 <= t - k < n:
                    vals[k, t - k] = stage(t - k, vals.pop((k - 1, t - k), None))

    for c in range(tq // CHUNK):
        sl = slice(c * CHUNK, (c + 1) * CHUNK)
        y_ref[0, sl, 0:A_WIDTH] = _gmlp_chunk(u_ref[0, sl, :], v_ref[0, sl, :], ws_all, bsm).astype(BF16)
    fns = {"na": [na_scores, na_probs, na_finish], "sw": [sw_scores, sw_probs, sw_finish]}
    items = []
    for c in range(tq // SW_BLOCK):
        items += [("na", 2 * c), ("na", 2 * c + 1), ("sw", c)]
    pipelined(len(items), [lambda t, v, k=k: fns[items[t][0]][k](items[t][1], v) for k in range(3)])


def _mixer(u, v, naq, nak, navt, swq, swk, swvt, cnak, cnavt, cswk, cswvt,
           ws_all, bsm, bias_tab, sw_mask, sink, *, tq):
    b, s, _ = u.shape
    d = A_WIDTH + NA_WIDTH + SW_Q_WIDTH
    lc = cnak.shape[1]
    tile = lambda w: pl.BlockSpec((1, tq, w), lambda i, j: (i, j, 0))
    full = lambda n, w: pl.BlockSpec((1, n, w), lambda i, j: (i, 0, 0))
    const = lambda shape: pl.BlockSpec(shape, lambda i, j: (0,) * len(shape))
    return pl.pallas_call(
        functools.partial(_mixer_kernel, tq=tq, seq=s),
        grid=(b, s // tq),
        in_specs=[tile(A_WIDTH), tile(A_WIDTH),
                  tile(NA_WIDTH), full(s, NA_WIDTH), full(NA_VT_ROWS, s),
                  tile(SW_Q_WIDTH), full(s, SW_KV_WIDTH), full(SW_VT_ROWS, s),
                  full(lc, NA_WIDTH), full(NA_VT_ROWS, lc), full(lc, SW_KV_WIDTH), full(SW_VT_ROWS, lc),
                  const(ws_all.shape), const(bsm.shape), const(bias_tab.shape), const(sw_mask.shape),
                  const(sink.shape)],
        out_specs=tile(d),
        out_shape=jax.ShapeDtypeStruct((b, s, d), BF16),
        scratch_shapes=[pltpu.VMEM((2, NA_VT_ROWS, s), BF16)],
        compiler_params=_params(2),
        name="mixer",
    )(u, v, naq, nak, navt, swq, swk, swvt, cnak, cnavt, cswk, cswvt,
      ws_all, bsm, bias_tab, sw_mask, sink)


def _ctx_mixer_kernel(u_ref, v_ref, naq_ref, nak_ref, navt_ref, swq_ref, swk_ref, swvt_ref,
                      ws_ref, bsm_ref, sink_ref, y_ref, *, lc):
    ws_all = ws_ref[...]
    bsm = bsm_ref[...]
    for c in range(lc // CHUNK):
        sl = slice(c * CHUNK, (c + 1) * CHUNK)
        y_ref[0, sl, 0:A_WIDTH] = _gmlp_chunk(u_ref[0, sl, :], v_ref[0, sl, :], ws_all, bsm).astype(BF16)

    out = _na_finish(_exp_t(_na_scores(naq_ref[0], nak_ref[0]))[0], navt_ref[0])
    y_ref[0, :, A_WIDTH:A_WIDTH + NA_WIDTH] = out.astype(BF16)

    slabs = _sw_finish(*_sw_probs(_sw_scores(swq_ref[0], swk_ref[0]), sink_ref[...]), swvt_ref[0])
    base = A_WIDTH + NA_WIDTH
    for j, slab in enumerate(slabs):
        y_ref[0, :, base + j * LANES:base + (j + 1) * LANES] = slab.astype(BF16)


def _ctx_mixer(u, v, naq, nak, navt, swq, swk, swvt, ws_all, bsm, sink):
    b, lc, _ = u.shape
    d = A_WIDTH + NA_WIDTH + SW_Q_WIDTH
    full = lambda n, w: pl.BlockSpec((1, n, w), lambda i: (i, 0, 0))
    const = lambda shape: pl.BlockSpec(shape, lambda i: (0,) * len(shape))
    return pl.pallas_call(
        functools.partial(_ctx_mixer_kernel, lc=lc),
        grid=(b,),
        in_specs=[full(lc, A_WIDTH), full(lc, A_WIDTH), full(lc, NA_WIDTH), full(lc, NA_WIDTH),
                  full(NA_VT_ROWS, lc), full(lc, SW_Q_WIDTH), full(lc, SW_KV_WIDTH), full(SW_VT_ROWS, lc),
                  const(ws_all.shape), const(bsm.shape), const(sink.shape)],
        out_specs=full(lc, d),
        out_shape=jax.ShapeDtypeStruct((b, lc, d), BF16),
        compiler_params=_params(1),
        name="ctx_mixer",
    )(u, v, naq, nak, navt, swq, swk, swvt, ws_all, bsm, sink)


def _out_ffn_kernel(y_ref, x_ref, mod_ref, wout_ref, l1g_ref, l1b_ref, w1_ref, w2_ref, l2g_ref, l2b_ref,
                    o_ref, *, alpha):
    tm = x_ref.shape[1]
    halves = [slice(0, tm // 2), slice(tm // 2, tm)]
    def out_proj(sl):
        return jnp.dot(y_ref[0, sl, :], wout_ref[...], preferred_element_type=F32)

    def norm1(sl, yo_h):
        x_h = _layer_norm(alpha * x_ref[0, sl, :] + mod_ref[0, 2:3, :] * yo_h, l1g_ref[...], l1b_ref[...])
        return x_h, (x_h * (1.0 + mod_ref[0, 4:5, :]) + mod_ref[0, 3:4, :]).astype(BF16)

    x0, h0 = norm1(halves[0], out_proj(halves[0]))
    yo1 = out_proj(halves[1])
    a0 = jnp.dot(h0, w1_ref[...], preferred_element_type=F32)
    x1, h1 = norm1(halves[1], yo1)
    a1 = jnp.dot(h1, w1_ref[...], preferred_element_type=F32)
    x, a = [x0, x1], [a0, a1]
    f = [jnp.dot(jnp.square(jnp.maximum(a_h, 0.0)).astype(BF16), w2_ref[...], preferred_element_type=F32)
         for a_h in a]
    for sl, x_h, f_h in zip(halves, x, f):
        o_ref[0, sl, :] = _layer_norm(alpha * x_h + mod_ref[0, 5:6, :] * f_h, l2g_ref[...], l2b_ref[...])


def _out_ffn(y, x, mod6, wout, l1g, l1b, w1, w2, l2g, l2b, *, tm, alpha):
    b, s, d = x.shape
    tile = pl.BlockSpec((1, tm, d), lambda i, j: (i, j, 0))
    const = lambda shape: pl.BlockSpec(shape, lambda i, j: (0,) * len(shape),
                                       pipeline_mode=pl.Buffered(1))
    return pl.pallas_call(
        functools.partial(_out_ffn_kernel, alpha=alpha),
        grid=(b, s // tm),
        in_specs=[tile, tile, pl.BlockSpec((1, 6, d), lambda i, j: (i, 0, 0)),
                  const(wout.shape), const((1, d)), const((1, d)),
                  const(w1.shape), const(w2.shape), const((1, d)), const((1, d))],
        out_specs=tile,
        out_shape=jax.ShapeDtypeStruct((b, s, d), F32),
        compiler_params=_params(2),
        name="out_ffn",
    )(y, x, mod6, wout, l1g, l1b, w1, w2, l2g, l2b)


def _sw_head_order():
    return [kv * SW_GROUP + j for j in range(SW_GROUP) for kv in range(SW_KV_HEADS)]


def _rope_tables(seq):
    pos = jnp.arange(seq)
    row_pos, col_pos = pos // GRID_W, pos % GRID_W
    inv_freq = ROPE_BASE ** (-jnp.arange(ROPE_FREQS, dtype=F32) / ROPE_FREQS)
    ang_r = row_pos.astype(F32)[:, None] * inv_freq
    ang_c = col_pos.astype(F32)[:, None] * inv_freq
    cos_head = jnp.concatenate([jnp.cos(ang_r), jnp.cos(ang_r), jnp.cos(ang_c), jnp.cos(ang_c)], axis=-1)
    sin_head = jnp.concatenate([-jnp.sin(ang_r), jnp.sin(ang_r), -jnp.sin(ang_c), jnp.sin(ang_c)], axis=-1)
    return jnp.tile(cos_head, (1, LANES // HEAD_DIM)), jnp.tile(sin_head, (1, LANES // HEAD_DIM))


def _na_bias_table(rpb):
    cq = np.arange(GRID_W)
    cs = np.clip(cq - NA_COLS // 2, 0, GRID_W - NA_COLS)
    col_ok = (cq[None, :] >= cs[:, None]) & (cq[None, :] < cs[:, None] + NA_COLS)
    ext = GRID_W - NA_COLS
    rpb_ext = jnp.concatenate([jnp.repeat(rpb[..., :1], ext, axis=-1), rpb,
                               jnp.repeat(rpb[..., -1:], ext, axis=-1)], axis=-1)
    rel = jnp.stack([rpb_ext[..., GRID_W - 1 - q:2 * GRID_W - 1 - q] for q in range(GRID_W)], axis=2)
    bias_col = jnp.where(col_ok[None, None], rel * LOG2E, NEG_INF)
    return bias_col.transpose(1, 3, 0, 2).reshape(2 * NA_ROWS - 1, GRID_W, NA_HEADS * GRID_W)


def _sw_mask_table():
    kj = np.arange(SW_BLOCK)[:, None]
    qi = np.arange(SW_BLOCK)[None, :]
    prev = np.where(kj >= qi, 0.0, NEG_INF)
    nxt = np.where(kj <= qi, 0.0, NEG_INF)
    none = np.full((SW_BLOCK, SW_BLOCK), NEG_INF)
    return jnp.asarray(np.stack([np.tile(m, (1, SW_HEADS)) for m in (prev, nxt, none)]), F32)


def kernel(x, c, ctx, c_ctx, w_mod, b_mod, w_in, a_ln_g, a_ln_b, a_ws, a_bs, na_rpb, sw_sink, w_out,
           ln1_g, ln1_b, w1, w2, ln2_g, ln2_b):
    depth = w_mod.shape[0]
    b, s, d = x.shape
    lc = ctx.shape[1]
    alpha = (2 * depth) ** 0.25
    tq = 512
    assert s % (2 * tq) == 0 and (b * lc) % tq == 0 and lc % CHUNK == 0
    ctx_nb = math.gcd(b, max(1, 2 * tq // lc))

    rows = -(-(b + 1) // MOD_ROWS_PAD) * MOD_ROWS_PAD
    cc = jnp.concatenate([c, c_ctx[None], jnp.zeros((rows - b - 1, d), F32)], axis=0)
    mod_all = _modulation(cc, w_mod, b_mod)

    q0 = 2 * A_WIDTH + 3 * NA_WIDTH
    nav0 = 2 * A_WIDTH + 2 * NA_WIDTH
    k0 = q0 + SW_Q_WIDTH
    heads = _sw_head_order()
    cos, sin = _rope_tables(s)
    sw_mask = _sw_mask_table()

    xc = ctx
    for layer in range(depth):
        mod6 = mod_all[layer, :b].reshape(b, 6, d)
        modc6 = jnp.broadcast_to(mod_all[layer, b].reshape(1, 6, d), (b, 6, d))
        wl = w_in[layer]
        w_main = jnp.concatenate(
            [wl[:, :nav0]] + [wl[:, q0 + h * HEAD_DIM:q0 + (h + 1) * HEAD_DIM] for h in heads]
            + [wl[:, k0:k0 + SW_KV_WIDTH]], axis=1).astype(BF16)
        w_vt = jnp.concatenate([wl[:, nav0:q0], wl[:, k0 + SW_KV_WIDTH:]], axis=1).T.astype(BF16)
        wo = w_out[layer]
        y0 = A_WIDTH + NA_WIDTH
        w_out_l = jnp.concatenate(
            [wo[:y0]] + [wo[y0 + h * HEAD_DIM:y0 + (h + 1) * HEAD_DIM] for h in heads], axis=0).astype(BF16)
        w1_l = w1[layer].astype(BF16)
        w2_l = w2[layer].astype(BF16)
        lng = a_ln_g[layer].reshape(1, A_WIDTH)
        lnb = a_ln_b[layer].reshape(1, A_WIDTH)
        ws_all = a_ws[layer].reshape(A_GROUPS * CHUNK, CHUNK).astype(BF16)
        bsm = jnp.repeat(a_bs[layer].T, HEAD_DIM, axis=1)
        bias_tab = _na_bias_table(na_rpb[layer])
        sink8 = sw_sink[layer] * LOG2E
        sink_blk = jnp.repeat(sink8, SW_BLOCK)[None, :]
        sink_ctx = jnp.repeat(sink8, lc)[None, :]
        l1g, l1b = ln1_g[layer].reshape(1, d), ln1_b[layer].reshape(1, d)
        l2g, l2b = ln2_g[layer].reshape(1, d), ln2_b[layer].reshape(1, d)

        u, v, naq, nak, navt, swq, swk, swvt = _in_projection(
            x, mod6, w_main, w_vt, lng, lnb, cos, sin, rope=True, tm=1024)
        last = layer == depth - 1
        ctx_proj = _in_projection(xc, modc6, w_main, w_vt, lng, lnb, cos, sin, rope=False, tm=lc,
                                  nb=ctx_nb, kv_only=last)
        if last:
            cnak, cnavt, cswk, cswvt = ctx_proj
        else:
            cu, cv, cnaq, cnak, cnavt, cswq, cswk, cswvt = ctx_proj

        y = _mixer(u, v, naq, nak, navt, swq, swk, swvt, cnak, cnavt, cswk, cswvt,
                   ws_all, bsm, bias_tab, sw_mask, sink_blk, tq=tq)
        x = _out_ffn(y, x, mod6, w_out_l, l1g, l1b, w1_l, w2_l, l2g, l2b, tm=512, alpha=alpha)

        if not last:
            yc = _ctx_mixer(cu, cv, cnaq, cnak, cnavt, cswq, cswk, cswvt, ws_all, bsm, sink_ctx)
            groups = b * lc // tq
            xc = _out_ffn(yc.reshape(groups, tq, d), xc.reshape(groups, tq, d), modc6[:groups], w_out_l, l1g, l1b,
                          w1_l, w2_l, l2g, l2b, tm=tq, alpha=alpha).reshape(b, lc, d)
    return x
```

```python
import functools
import math

import numpy as np
import jax
import jax.numpy as jnp
from jax import lax
from jax.experimental import pallas as pl
from jax.experimental.pallas import tpu as pltpu

F32 = jnp.float32
BF16 = jnp.bfloat16

D_MODEL = 1024
HEAD_DIM = 64
GRID_W = 64
CHUNK = 128
A_GROUPS = 4
A_WIDTH = 256
NA_HEADS = 4
NA_WIDTH = 256
NA_ROWS = 8
NA_COLS = 16
SW_KV_HEADS = 2
SW_GROUP = 4
SW_HEADS = SW_KV_HEADS * SW_GROUP
SW_Q_WIDTH = 512
SW_KV_WIDTH = 128
SW_BLOCK = 128
IN_WIDTH = 2048
VT_WIDTH = NA_WIDTH + SW_KV_WIDTH
ONES_ROWS = 16
NA_VT_ROWS = NA_WIDTH + ONES_ROWS
SW_VT_ROWS = SW_KV_WIDTH + ONES_ROWS
ROPE_BASE = 10000.0
ROPE_FREQS = 16
LN_EPS = 1e-5
NEG_INF = -1e30
LOG2E = math.log2(math.e)
Q_SCALE = HEAD_DIM ** -0.5 * LOG2E

FFN_SUB = 256
LANES = 128
MOD_ROWS_PAD = 8
VMEM_LIMIT = 56 * 1024 * 1024


def _layer_norm(x, g, b):
    mu = jnp.mean(x, axis=-1, keepdims=True)
    var = jnp.mean(jnp.square(x - mu), axis=-1, keepdims=True)
    return (x - mu) * lax.rsqrt(var + LN_EPS) * g + b


def _params(n_grid):
    return pltpu.CompilerParams(dimension_semantics=("arbitrary",) * n_grid,
                                vmem_limit_bytes=VMEM_LIMIT)


def _pipelined(n, stages):
    vals = {}
    for t in range(n + len(stages) - 1):
        for k, stage in reversed(list(enumerate(stages))):
            if 0 <= t - k < n:
                vals[k, t - k] = stage(t - k, vals.pop((k - 1, t - k), None))


def _nt_dot(a, b):
    return lax.dot_general(a, b, (((1,), (1,)), ((), ())), preferred_element_type=F32)


def _mod_kernel(c_ref, w_ref, b_ref, o_ref):
    a = jax.nn.silu(c_ref[...]).astype(BF16)
    o_ref[0] = jnp.dot(a, w_ref[0].astype(BF16), preferred_element_type=F32) + b_ref[0]


def _modulation(cc, w_mod, b_mod, tn=1024):
    depth, d, n = w_mod.shape
    rows = cc.shape[0]
    return pl.pallas_call(
        _mod_kernel,
        grid=(depth, n // tn),
        in_specs=[pl.BlockSpec((rows, d), lambda l, j: (0, 0)),
                  pl.BlockSpec((1, d, tn), lambda l, j: (l, 0, j)),
                  pl.BlockSpec((1, 1, tn), lambda l, j: (l, 0, j))],
        out_specs=pl.BlockSpec((1, rows, tn), lambda l, j: (l, 0, j)),
        out_shape=jax.ShapeDtypeStruct((depth, rows, n), F32),
        compiler_params=_params(2),
        name="modulation",
    )(cc, w_mod, b_mod.reshape(depth, 1, n))


def _inproj_kernel(x_ref, mod_ref, w_ref, wvt_ref, lng_ref, lnb_ref, cos_ref, sin_ref,
                   *out_refs, rope, kv_only):
    if kv_only:
        nak_ref, navt_ref, swk_ref, swvt_ref = out_refs
    else:
        u_ref, v_ref, naq_ref, nak_ref, navt_ref, swq_ref, swk_ref, swvt_ref = out_refs
    nb, tm, d = x_ref.shape
    x = x_ref[...].reshape(nb * tm, d)
    h = (x * (1.0 + mod_ref[0, 1:2, :]) + mod_ref[0, 0:1, :]).astype(BF16)

    def put(ref, val):
        ref[...] = val.astype(BF16).reshape(ref.shape)

    def proj(lo, hi):
        return jnp.dot(h, w_ref[:, lo:hi], preferred_element_type=F32)

    if rope:
        cos = cos_ref[...]
        sin = sin_ref[...]
        first = (lax.broadcasted_iota(jnp.int32, (1, LANES), 1) % 32) < 16

        def rot(t):
            partner = jnp.where(first, pltpu.roll(t, LANES - 16, axis=1), pltpu.roll(t, 16, axis=1))
            return t * cos + partner * sin
    else:
        def rot(t):
            return t

    if not kv_only:
        put(u_ref, jax.nn.gelu(proj(0, 256)))
        put(v_ref, _layer_norm(jax.nn.gelu(proj(256, 512)), lng_ref[...], lnb_ref[...]))
        put(naq_ref, proj(512, 768) * Q_SCALE)
    put(nak_ref, proj(768, 1024))
    if kv_only:
        put(swk_ref, rot(proj(1536, 1664)))
    else:
        qk = proj(1024, 1664)
        for j in range(SW_Q_WIDTH // LANES):
            sl = slice(j * LANES, (j + 1) * LANES)
            put(swq_ref.at[:, :, sl], rot(qk[:, sl]) * Q_SCALE)
        put(swk_ref, rot(qk[:, SW_Q_WIDTH:]))
    vt = _nt_dot(wvt_ref[...], h).astype(BF16)
    ones = jnp.ones((ONES_ROWS, tm), BF16)
    for i in range(nb):
        cols = slice(i * tm, (i + 1) * tm)
        navt_ref[i, :NA_WIDTH, :] = vt[:NA_WIDTH, cols]
        navt_ref[i, NA_WIDTH:, :] = ones
        swvt_ref[i, :SW_KV_WIDTH, :] = vt[NA_WIDTH:, cols]
        swvt_ref[i, SW_KV_WIDTH:, :] = ones


def _in_projection(x, mod6, w_main, w_vt, ln_g, ln_b, cos, sin, *, rope, tm, nb=1, kv_only=False):
    b, s, d = x.shape
    assert b % nb == 0 and not (rope and nb > 1)
    tile = lambda w: pl.BlockSpec((nb, tm, w), lambda i, j: (i, j, 0))
    tile_t = lambda w: pl.BlockSpec((nb, w, tm), lambda i, j: (i, 0, j))
    const = lambda shape: pl.BlockSpec(shape, lambda i, j: (0,) * len(shape))
    nat = lambda w: jax.ShapeDtypeStruct((b, s, w), BF16)
    tra = lambda w: jax.ShapeDtypeStruct((b, w, s), BF16)
    outs = [(tile(A_WIDTH), nat(A_WIDTH)), (tile(A_WIDTH), nat(A_WIDTH)), (tile(NA_WIDTH), nat(NA_WIDTH)),
            (tile(NA_WIDTH), nat(NA_WIDTH)), (tile_t(NA_VT_ROWS), tra(NA_VT_ROWS)),
            (tile(SW_Q_WIDTH), nat(SW_Q_WIDTH)), (tile(SW_KV_WIDTH), nat(SW_KV_WIDTH)),
            (tile_t(SW_VT_ROWS), tra(SW_VT_ROWS))]
    if kv_only:
        outs = [outs[3], outs[4], outs[6], outs[7]]
    return pl.pallas_call(
        functools.partial(_inproj_kernel, rope=rope, kv_only=kv_only),
        grid=(b // nb, s // tm),
        in_specs=[tile(d),
                  pl.BlockSpec((1, 6, d), lambda i, j: (i * nb, 0, 0)),
                  const(w_main.shape), const(w_vt.shape),
                  const((1, A_WIDTH)), const((1, A_WIDTH)),
                  pl.BlockSpec((tm, LANES), lambda i, j: (j, 0)),
                  pl.BlockSpec((tm, LANES), lambda i, j: (j, 0))],
        out_specs=[spec for spec, _ in outs],
        out_shape=[shape for _, shape in outs],
        compiler_params=_params(2),
        name="in_projection_rope" if rope else "in_projection",
    )(x, mod6, w_main, w_vt, ln_g, ln_b, cos, sin)


def _gmlp_chunk(u, v, ws_all, bsm):
    m = jnp.dot(ws_all, v, preferred_element_type=F32)
    group = lax.broadcasted_iota(jnp.int32, (1, A_WIDTH), 1) // HEAD_DIM
    mixed = m[0:CHUNK]
    for g in range(1, A_GROUPS):
        mixed = jnp.where(group == g, m[g * CHUNK:(g + 1) * CHUNK], mixed)
    return u.astype(F32) * (mixed + bsm)


def _stack_na_q(q):
    head = lax.broadcasted_iota(jnp.int32, (1, NA_WIDTH), 1) // HEAD_DIM
    zero = jnp.zeros_like(q)
    return jnp.concatenate([jnp.where(head == h, q, zero) for h in range(NA_HEADS)], axis=0)


def _stack_sw_q(q):
    low = lax.broadcasted_iota(jnp.int32, (1, LANES), 1) < HEAD_DIM
    slabs = [q[:, j * LANES:(j + 1) * LANES] for j in range(SW_GROUP)]
    zero = jnp.zeros_like(slabs[0])
    return jnp.concatenate([jnp.where(low, qj, zero) for qj in slabs]
                           + [jnp.where(low, zero, qj) for qj in slabs], axis=0)


def _exp_t(s_t, sink=None):
    m = jnp.max(s_t, axis=0, keepdims=True)
    if sink is not None:
        m = jnp.maximum(m, sink)
    return jnp.exp2((s_t - m).astype(BF16)), m


def _na_scores(q, kcat):
    return _nt_dot(kcat, _stack_na_q(q))


def _na_finish(e, v_t):
    n = e.shape[1] // NA_HEADS
    o_t = jnp.dot(v_t, e, preferred_element_type=F32)
    o = (o_t[:NA_WIDTH] * (1.0 / o_t[NA_WIDTH:NA_WIDTH + 1])).T
    head = lax.broadcasted_iota(jnp.int32, (1, NA_WIDTH), 1) // HEAD_DIM
    out = o[0:n]
    for h in range(1, NA_HEADS):
        out = jnp.where(head == h, o[h * n:(h + 1) * n], out)
    return out


def _sw_scores(q, kcat):
    return _nt_dot(kcat, _stack_sw_q(q))


def _sw_probs(s_t, sink):
    e, m = _exp_t(s_t, sink)
    return e, jnp.exp2(sink - m)


def _sw_finish(e, sink_e, v_t):
    n = e.shape[1] // SW_HEADS
    ones = v_t[SW_KV_WIDTH:]
    o_kv = []
    for kv in range(SW_KV_HEADS):
        cols = slice(kv * SW_GROUP * n, (kv + 1) * SW_GROUP * n)
        v_kv = jnp.concatenate([v_t[kv * HEAD_DIM:(kv + 1) * HEAD_DIM], ones], axis=0)
        o_t = jnp.dot(v_kv, e[:, cols], preferred_element_type=F32)
        inv_l = 1.0 / (o_t[HEAD_DIM:HEAD_DIM + 1] + sink_e[:, cols])
        o_kv.append(o_t[:HEAD_DIM] * inv_l)
    return [jnp.concatenate([o[:, j * n:(j + 1) * n] for o in o_kv], axis=0).T for j in range(SW_GROUP)]


def _mixer_kernel(u_ref, v_ref, naq_ref, nak_ref, navt_ref, swq_ref, swk_ref, swvt_ref,
                  cnak_ref, cnavt_ref, cswk_ref, cswvt_ref,
                  ws_ref, bsm_ref, bias_ref, swmask_ref, sink_ref,
                  y_ref, navt2_scr, *, tq, seq):
    tile = pl.program_id(1)
    rows_per_tile = tq // GRID_W
    n_rows = seq // GRID_W
    win_keys = NA_ROWS * GRID_W

    @pl.when(tile == 0)
    def _():
        navt2_scr[0] = navt_ref[0]
        navt2_scr[1, :, 0:seq - GRID_W] = navt_ref[0, :, GRID_W:seq]
        navt2_scr[1, :, seq - GRID_W:seq] = jnp.zeros((NA_VT_ROWS, GRID_W), BF16)

    ws_all = ws_ref[...]
    bsm = bsm_ref[...]

    def na_window(i):
        r = tile * rows_per_tile + i
        rs = jnp.clip(r - NA_ROWS // 2, 0, n_rows - NA_ROWS)
        return rs, r - rs

    def na_scores(i, _):
        rs, off = na_window(i)
        kstart = pl.multiple_of(rs * GRID_W, GRID_W)
        kcat = jnp.concatenate([nak_ref[0, pl.ds(kstart, win_keys), :], cnak_ref[0]], axis=0)
        s_t = _na_scores(naq_ref[0, i * GRID_W:(i + 1) * GRID_W, :], kcat)
        parts = [s_t[j * GRID_W:(j + 1) * GRID_W] + bias_ref[j + NA_ROWS - 1 - off] for j in range(NA_ROWS)]
        return jnp.concatenate(parts + [s_t[win_keys:]], axis=0)

    def na_probs(i, s_t):
        return _exp_t(s_t)[0]

    def na_finish(i, e):
        rs, _ = na_window(i)
        par = rs % 2
        vstart = pl.multiple_of((rs - par) * GRID_W, LANES)
        v_t = jnp.concatenate([navt2_scr[par, :, pl.ds(vstart, win_keys)], cnavt_ref[0]], axis=1)
        y_ref[0, i * GRID_W:(i + 1) * GRID_W, A_WIDTH:A_WIDTH + NA_WIDTH] = _na_finish(e, v_t).astype(BF16)

    n_blocks = seq // SW_BLOCK
    sink = sink_ref[...]

    def sw_starts(i):
        t0 = (tile * (tq // SW_BLOCK) + i) * SW_BLOCK
        return (pl.multiple_of(jnp.maximum(t0 - SW_BLOCK, 0), SW_BLOCK), pl.multiple_of(t0, SW_BLOCK),
                pl.multiple_of(jnp.minimum(t0 + SW_BLOCK, seq - SW_BLOCK), SW_BLOCK))

    def sw_scores(i, _):
        bi = tile * (tq // SW_BLOCK) + i
        kcat = jnp.concatenate([swk_ref[0, pl.ds(st, SW_BLOCK), :] for st in sw_starts(i)] + [cswk_ref[0]], axis=0)
        s_t = _sw_scores(swq_ref[0, i * SW_BLOCK:(i + 1) * SW_BLOCK, :], kcat)
        prev_mask = swmask_ref[jnp.where(bi > 0, 0, 2)]
        next_mask = swmask_ref[jnp.where(bi < n_blocks - 1, 1, 2)]
        return jnp.concatenate([s_t[0:SW_BLOCK] + prev_mask, s_t[SW_BLOCK:2 * SW_BLOCK],
                                s_t[2 * SW_BLOCK:3 * SW_BLOCK] + next_mask, s_t[3 * SW_BLOCK:]], axis=0)

    def sw_probs(i, s_t):
        return _sw_probs(s_t, sink)

    def sw_finish(i, probs):
        v_t = jnp.concatenate([swvt_ref[0, :, pl.ds(st, SW_BLOCK)] for st in sw_starts(i)] + [cswvt_ref[0]], axis=1)
        base = A_WIDTH + NA_WIDTH
        for j, slab in enumerate(_sw_finish(*probs, v_t)):
            y_ref[0, i * SW_BLOCK:(i + 1) * SW_BLOCK, base + j * LANES:base + (j + 1) * LANES] = slab.astype(BF16)

    for c in range(tq // CHUNK):
        sl = slice(c * CHUNK, (c + 1) * CHUNK)
        y_ref[0, sl, 0:A_WIDTH] = _gmlp_chunk(u_ref[0, sl, :], v_ref[0, sl, :], ws_all, bsm).astype(BF16)
    fns = {"na": [na_scores, na_probs, na_finish], "sw": [sw_scores, sw_probs, sw_finish]}
    items = []
    for c in range(tq // SW_BLOCK):
        items += [("na", 2 * c), ("na", 2 * c + 1), ("sw", c)]
    _pipelined(len(items), [lambda t, v, k=k: fns[items[t][0]][k](items[t][1], v) for k in range(3)])


def _mixer(u, v, naq, nak, navt, swq, swk, swvt, cnak, cnavt, cswk, cswvt,
           ws_all, bsm, bias_tab, sw_mask, sink, *, tq):
    b, s, _ = u.shape
    d = A_WIDTH + NA_WIDTH + SW_Q_WIDTH
    lc = cnak.shape[1]
    tile = lambda w: pl.BlockSpec((1, tq, w), lambda i, j: (i, j, 0))
    full = lambda n, w: pl.BlockSpec((1, n, w), lambda i, j: (i, 0, 0))
    const = lambda shape: pl.BlockSpec(shape, lambda i, j: (0,) * len(shape))
    return pl.pallas_call(
        functools.partial(_mixer_kernel, tq=tq, seq=s),
        grid=(b, s // tq),
        in_specs=[tile(A_WIDTH), tile(A_WIDTH),
                  tile(NA_WIDTH), full(s, NA_WIDTH), full(NA_VT_ROWS, s),
                  tile(SW_Q_WIDTH), full(s, SW_KV_WIDTH), full(SW_VT_ROWS, s),
                  full(lc, NA_WIDTH), full(NA_VT_ROWS, lc), full(lc, SW_KV_WIDTH), full(SW_VT_ROWS, lc),
                  const(ws_all.shape), const(bsm.shape), const(bias_tab.shape), const(sw_mask.shape),
                  const(sink.shape)],
        out_specs=tile(d),
        out_shape=jax.ShapeDtypeStruct((b, s, d), BF16),
        scratch_shapes=[pltpu.VMEM((2, NA_VT_ROWS, s), BF16)],
        compiler_params=_params(2),
        name="mixer",
    )(u, v, naq, nak, navt, swq, swk, swvt, cnak, cnavt, cswk, cswvt,
      ws_all, bsm, bias_tab, sw_mask, sink)


def _ctx_mixer_kernel(u_ref, v_ref, naq_ref, nak_ref, navt_ref, swq_ref, swk_ref, swvt_ref,
                      ws_ref, bsm_ref, sink_ref, y_ref, *, lc):
    ws_all = ws_ref[...]
    bsm = bsm_ref[...]
    for c in range(lc // CHUNK):
        sl = slice(c * CHUNK, (c + 1) * CHUNK)
        y_ref[0, sl, 0:A_WIDTH] = _gmlp_chunk(u_ref[0, sl, :], v_ref[0, sl, :], ws_all, bsm).astype(BF16)

    out = _na_finish(_exp_t(_na_scores(naq_ref[0], nak_ref[0]))[0], navt_ref[0])
    y_ref[0, :, A_WIDTH:A_WIDTH + NA_WIDTH] = out.astype(BF16)

    slabs = _sw_finish(*_sw_probs(_sw_scores(swq_ref[0], swk_ref[0]), sink_ref[...]), swvt_ref[0])
    base = A_WIDTH + NA_WIDTH
    for j, slab in enumerate(slabs):
        y_ref[0, :, base + j * LANES:base + (j + 1) * LANES] = slab.astype(BF16)


def _ctx_mixer(u, v, naq, nak, navt, swq, swk, swvt, ws_all, bsm, sink):
    b, lc, _ = u.shape
    d = A_WIDTH + NA_WIDTH + SW_Q_WIDTH
    full = lambda n, w: pl.BlockSpec((1, n, w), lambda i: (i, 0, 0))
    const = lambda shape: pl.BlockSpec(shape, lambda i: (0,) * len(shape))
    return pl.pallas_call(
        functools.partial(_ctx_mixer_kernel, lc=lc),
        grid=(b,),
        in_specs=[full(lc, A_WIDTH), full(lc, A_WIDTH), full(lc, NA_WIDTH), full(lc, NA_WIDTH),
                  full(NA_VT_ROWS, lc), full(lc, SW_Q_WIDTH), full(lc, SW_KV_WIDTH), full(SW_VT_ROWS, lc),
                  const(ws_all.shape), const(bsm.shape), const(sink.shape)],
        out_specs=full(lc, d),
        out_shape=jax.ShapeDtypeStruct((b, lc, d), BF16),
        compiler_params=_params(1),
        name="ctx_mixer",
    )(u, v, naq, nak, navt, swq, swk, swvt, ws_all, bsm, sink)


def _out_ffn_kernel(y_ref, x_ref, mod_ref, wout_ref, l1g_ref, l1b_ref, w1_ref, w2_ref, l2g_ref, l2b_ref,
                    o_ref, *, alpha):
    tm = x_ref.shape[1]
    rows = [slice(i * FFN_SUB, (i + 1) * FFN_SUB) for i in range(tm // FFN_SUB)]

    def out_proj(i, _):
        return jnp.dot(y_ref[0, rows[i], :], wout_ref[...], preferred_element_type=F32)

    def norm1(i, yo):
        x = _layer_norm(alpha * x_ref[0, rows[i], :] + mod_ref[0, 2:3, :] * yo, l1g_ref[...], l1b_ref[...])
        return x, (x * (1.0 + mod_ref[0, 4:5, :]) + mod_ref[0, 3:4, :]).astype(BF16)

    def up(i, xh):
        return xh[0], jnp.dot(xh[1], w1_ref[...], preferred_element_type=F32)

    def down(i, xa):
        a = jnp.square(jnp.maximum(xa[1], 0.0)).astype(BF16)
        return xa[0], jnp.dot(a, w2_ref[...], preferred_element_type=F32)

    def norm2(i, xf):
        o_ref[0, rows[i], :] = _layer_norm(alpha * xf[0] + mod_ref[0, 5:6, :] * xf[1], l2g_ref[...], l2b_ref[...])

    _pipelined(len(rows), [out_proj, norm1, up, down, norm2])


def _out_ffn(y, x, mod6, wout, l1g, l1b, w1, w2, l2g, l2b, *, tm, alpha):
    b, s, d = x.shape
    tile = pl.BlockSpec((1, tm, d), lambda i, j: (i, j, 0))
    const = lambda shape: pl.BlockSpec(shape, lambda i, j: (0,) * len(shape),
                                       pipeline_mode=pl.Buffered(1))
    return pl.pallas_call(
        functools.partial(_out_ffn_kernel, alpha=alpha),
        grid=(b, s // tm),
        in_specs=[tile, tile, pl.BlockSpec((1, 6, d), lambda i, j: (i, 0, 0)),
                  const(wout.shape), const((1, d)), const((1, d)),
                  const(w1.shape), const(w2.shape), const((1, d)), const((1, d))],
        out_specs=tile,
        out_shape=jax.ShapeDtypeStruct((b, s, d), F32),
        compiler_params=_params(2),
        name="out_ffn",
    )(y, x, mod6, wout, l1g, l1b, w1, w2, l2g, l2b)


def _sw_head_order():
    return [kv * SW_GROUP + j for j in range(SW_GROUP) for kv in range(SW_KV_HEADS)]


def _rope_tables(seq):
    pos = jnp.arange(seq)
    row_pos, col_pos = pos // GRID_W, pos % GRID_W
    inv_freq = ROPE_BASE ** (-jnp.arange(ROPE_FREQS, dtype=F32) / ROPE_FREQS)
    ang_r = row_pos.astype(F32)[:, None] * inv_freq
    ang_c = col_pos.astype(F32)[:, None] * inv_freq
    cos_head = jnp.concatenate([jnp.cos(ang_r), jnp.cos(ang_r), jnp.cos(ang_c), jnp.cos(ang_c)], axis=-1)
    sin_head = jnp.concatenate([-jnp.sin(ang_r), jnp.sin(ang_r), -jnp.sin(ang_c), jnp.sin(ang_c)], axis=-1)
    return jnp.tile(cos_head, (1, LANES // HEAD_DIM)), jnp.tile(sin_head, (1, LANES // HEAD_DIM))


def _na_bias_table(rpb):
    cq = np.arange(GRID_W)
    cs = np.clip(cq - NA_COLS // 2, 0, GRID_W - NA_COLS)
    col_ok = (cq[None, :] >= cs[:, None]) & (cq[None, :] < cs[:, None] + NA_COLS)
    ext = GRID_W - NA_COLS
    rpb_ext = jnp.concatenate([jnp.repeat(rpb[..., :1], ext, axis=-1), rpb,
                               jnp.repeat(rpb[..., -1:], ext, axis=-1)], axis=-1)
    rel = jnp.stack([rpb_ext[..., GRID_W - 1 - q:2 * GRID_W - 1 - q] for q in range(GRID_W)], axis=2)
    bias_col = jnp.where(col_ok[None, None], rel * LOG2E, NEG_INF)
    return bias_col.transpose(1, 3, 0, 2).reshape(2 * NA_ROWS - 1, GRID_W, NA_HEADS * GRID_W)


def _sw_mask_table():
    kj = np.arange(SW_BLOCK)[:, None]
    qi = np.arange(SW_BLOCK)[None, :]
    prev = np.where(kj >= qi, 0.0, NEG_INF)
    nxt = np.where(kj <= qi, 0.0, NEG_INF)
    none = np.full((SW_BLOCK, SW_BLOCK), NEG_INF)
    return jnp.asarray(np.stack([np.tile(m, (1, SW_HEADS)) for m in (prev, nxt, none)]), F32)


def kernel(x, c, ctx, c_ctx, w_mod, b_mod, w_in, a_ln_g, a_ln_b, a_ws, a_bs, na_rpb, sw_sink, w_out,
           ln1_g, ln1_b, w1, w2, ln2_g, ln2_b):
    depth = w_mod.shape[0]
    b, s, d = x.shape
    lc = ctx.shape[1]
    alpha = (2 * depth) ** 0.25
    tq = 512
    tm = 1024
    assert s % tm == 0 and tm % tq == 0 and (b * lc) % tm == 0 and lc % CHUNK == 0
    ctx_nb = math.gcd(b, max(1, tm // lc))

    rows = -(-(b + 1) // MOD_ROWS_PAD) * MOD_ROWS_PAD
    cc = jnp.concatenate([c, c_ctx[None], jnp.zeros((rows - b - 1, d), F32)], axis=0)
    mod_all = _modulation(cc, w_mod, b_mod)

    q0 = 2 * A_WIDTH + 3 * NA_WIDTH
    nav0 = 2 * A_WIDTH + 2 * NA_WIDTH
    k0 = q0 + SW_Q_WIDTH
    heads = _sw_head_order()
    cos, sin = _rope_tables(s)
    sw_mask = _sw_mask_table()

    xc = ctx
    for layer in range(depth):
        mod6 = mod_all[layer, :b].reshape(b, 6, d)
        modc6 = jnp.broadcast_to(mod_all[layer, b].reshape(1, 6, d), (b, 6, d))
        wl = w_in[layer]
        w_main = jnp.concatenate(
            [wl[:, :nav0]] + [wl[:, q0 + h * HEAD_DIM:q0 + (h + 1) * HEAD_DIM] for h in heads]
            + [wl[:, k0:k0 + SW_KV_WIDTH]], axis=1).astype(BF16)
        w_vt = jnp.concatenate([wl[:, nav0:q0], wl[:, k0 + SW_KV_WIDTH:]], axis=1).T.astype(BF16)
        wo = w_out[layer]
        y0 = A_WIDTH + NA_WIDTH
        w_out_l = jnp.concatenate(
            [wo[:y0]] + [wo[y0 + h * HEAD_DIM:y0 + (h + 1) * HEAD_DIM] for h in heads], axis=0).astype(BF16)
        w1_l = w1[layer].astype(BF16)
        w2_l = w2[layer].astype(BF16)
        lng = a_ln_g[layer].reshape(1, A_WIDTH)
        lnb = a_ln_b[layer].reshape(1, A_WIDTH)
        ws_all = a_ws[layer].reshape(A_GROUPS * CHUNK, CHUNK).astype(BF16)
        bsm = jnp.repeat(a_bs[layer].T, HEAD_DIM, axis=1)
        bias_tab = _na_bias_table(na_rpb[layer])
        sink8 = sw_sink[layer] * LOG2E
        sink_blk = jnp.repeat(sink8, SW_BLOCK)[None, :]
        sink_ctx = jnp.repeat(sink8, lc)[None, :]
        l1g, l1b = ln1_g[layer].reshape(1, d), ln1_b[layer].reshape(1, d)
        l2g, l2b = ln2_g[layer].reshape(1, d), ln2_b[layer].reshape(1, d)

        u, v, naq, nak, navt, swq, swk, swvt = _in_projection(
            x, mod6, w_main, w_vt, lng, lnb, cos, sin, rope=True, tm=tm)
        last = layer == depth - 1
        ctx_proj = _in_projection(xc, modc6, w_main, w_vt, lng, lnb, cos, sin, rope=False, tm=lc,
                                  nb=ctx_nb, kv_only=last)
        if last:
            cnak, cnavt, cswk, cswvt = ctx_proj
        else:
            cu, cv, cnaq, cnak, cnavt, cswq, cswk, cswvt = ctx_proj

        y = _mixer(u, v, naq, nak, navt, swq, swk, swvt, cnak, cnavt, cswk, cswvt,
                   ws_all, bsm, bias_tab, sw_mask, sink_blk, tq=tq)
        x = _out_ffn(y, x, mod6, w_out_l, l1g, l1b, w1_l, w2_l, l2g, l2b, tm=tm, alpha=alpha)

        if not last:
            yc = _ctx_mixer(cu, cv, cnaq, cnak, cnavt, cswq, cswk, cswvt, ws_all, bsm, sink_ctx)
            groups = b * lc // tm
            xc = _out_ffn(yc.reshape(groups, tm, d), xc.reshape(groups, tm, d), modc6[:groups], w_out_l, l1g, l1b,
                          w1_l, w2_l, l2g, l2b, tm=tm, alpha=alpha).reshape(b, lc, d)
    return x
```

```python
import functools
import math

import numpy as np
import jax
import jax.numpy as jnp
from jax import lax
from jax.experimental import pallas as pl
from jax.experimental.pallas import tpu as pltpu

F32 = jnp.float32
BF16 = jnp.bfloat16

D_MODEL = 1024
HEAD_DIM = 64
GRID_W = 64
CHUNK = 128
A_GROUPS = 4
A_WIDTH = 256
NA_HEADS = 4
NA_WIDTH = 256
NA_ROWS = 8
NA_COLS = 16
SW_KV_HEADS = 2
SW_GROUP = 4
SW_HEADS = SW_KV_HEADS * SW_GROUP
SW_Q_WIDTH = 512
SW_KV_WIDTH = 128
SW_BLOCK = 128
IN_WIDTH = 2048
VT_WIDTH = NA_WIDTH + SW_KV_WIDTH
ONES_ROWS = 16
NA_VT_ROWS = NA_WIDTH + ONES_ROWS
SW_VT_ROWS = SW_KV_WIDTH + ONES_ROWS
ROPE_BASE = 10000.0
ROPE_FREQS = 16
LN_EPS = 1e-5
NEG_INF = -1e30
LOG2E = math.log2(math.e)
Q_SCALE = HEAD_DIM ** -0.5 * LOG2E

FFN_SUB = 256
LANES = 128
MOD_ROWS_PAD = 8
VMEM_LIMIT = 56 * 1024 * 1024


def _layer_norm(x, g, b):
    mu = jnp.mean(x, axis=-1, keepdims=True)
    var = jnp.mean(jnp.square(x - mu), axis=-1, keepdims=True)
    return (x - mu) * lax.rsqrt(var + LN_EPS) * g + b


def _params(n_grid):
    return pltpu.CompilerParams(dimension_semantics=("arbitrary",) * n_grid,
                                vmem_limit_bytes=VMEM_LIMIT)


def _pipelined(n, stages):
    vals = {}
    for t in range(n + len(stages) - 1):
        for k, stage in reversed(list(enumerate(stages))):
            if 0 <= t - k < n:
                vals[k, t - k] = stage(t - k, vals.pop((k - 1, t - k), None))


def _nt_dot(a, b):
    return lax.dot_general(a, b, (((1,), (1,)), ((), ())), preferred_element_type=F32)


def _mod_kernel(c_ref, w_ref, b_ref, o_ref):
    a = jax.nn.silu(c_ref[...]).astype(BF16)
    o_ref[0] = jnp.dot(a, w_ref[0].astype(BF16), preferred_element_type=F32) + b_ref[0]


def _modulation(cc, w_mod, b_mod, tn=1024):
    depth, d, n = w_mod.shape
    rows = cc.shape[0]
    return pl.pallas_call(
        _mod_kernel,
        grid=(depth, n // tn),
        in_specs=[pl.BlockSpec((rows, d), lambda l, j: (0, 0)),
                  pl.BlockSpec((1, d, tn), lambda l, j: (l, 0, j)),
                  pl.BlockSpec((1, 1, tn), lambda l, j: (l, 0, j))],
        out_specs=pl.BlockSpec((1, rows, tn), lambda l, j: (l, 0, j)),
        out_shape=jax.ShapeDtypeStruct((depth, rows, n), F32),
        compiler_params=_params(2),
        name="modulation",
    )(cc, w_mod, b_mod.reshape(depth, 1, n))


def _inproj_kernel(x_ref, mod_ref, w_ref, wvt_ref, lng_ref, lnb_ref, cos_ref, sin_ref,
                   *out_refs, rope, kv_only):
    if kv_only:
        nak_ref, navt_ref, swk_ref, swvt_ref = out_refs
    else:
        u_ref, v_ref, naq_ref, nak_ref, navt_ref, swq_ref, swk_ref, swvt_ref = out_refs
    nb, tm, d = x_ref.shape
    x = x_ref[...].reshape(nb * tm, d)
    h = (x * (1.0 + mod_ref[0, 1:2, :]) + mod_ref[0, 0:1, :]).astype(BF16)

    def put(ref, val):
        ref[...] = val.astype(BF16).reshape(ref.shape)

    def proj(lo, hi):
        return jnp.dot(h, w_ref[:, lo:hi], preferred_element_type=F32)

    if rope:
        cos = cos_ref[...]
        sin = sin_ref[...]
        first = (lax.broadcasted_iota(jnp.int32, (1, LANES), 1) % 32) < 16

        def rot(t):
            partner = jnp.where(first, pltpu.roll(t, LANES - 16, axis=1), pltpu.roll(t, 16, axis=1))
            return t * cos + partner * sin
    else:
        def rot(t):
            return t

    if not kv_only:
        put(u_ref, jax.nn.gelu(proj(0, 256)))
        put(v_ref, _layer_norm(jax.nn.gelu(proj(256, 512)), lng_ref[...], lnb_ref[...]))
        put(naq_ref, proj(512, 768) * Q_SCALE)
    put(nak_ref, proj(768, 1024))
    if kv_only:
        put(swk_ref, rot(proj(1536, 1664)))
    else:
        qk = proj(1024, 1664)
        for j in range(SW_Q_WIDTH // LANES):
            sl = slice(j * LANES, (j + 1) * LANES)
            put(swq_ref.at[:, :, sl], rot(qk[:, sl]) * Q_SCALE)
        put(swk_ref, rot(qk[:, SW_Q_WIDTH:]))
    vt = _nt_dot(wvt_ref[...], h).astype(BF16)
    ones = jnp.ones((ONES_ROWS, tm), BF16)
    for i in range(nb):
        cols = slice(i * tm, (i + 1) * tm)
        navt_ref[i, :NA_WIDTH, :] = vt[:NA_WIDTH, cols]
        navt_ref[i, NA_WIDTH:, :] = ones
        swvt_ref[i, :SW_KV_WIDTH, :] = vt[NA_WIDTH:, cols]
        swvt_ref[i, SW_KV_WIDTH:, :] = ones


def _in_projection(x, mod6, w_main, w_vt, ln_g, ln_b, cos, sin, *, rope, tm, nb=1, kv_only=False):
    b, s, d = x.shape
    assert b % nb == 0 and not (rope and nb > 1)
    tile = lambda w: pl.BlockSpec((nb, tm, w), lambda i, j: (i, j, 0))
    tile_t = lambda w: pl.BlockSpec((nb, w, tm), lambda i, j: (i, 0, j))
    const = lambda shape: pl.BlockSpec(shape, lambda i, j: (0,) * len(shape))
    nat = lambda w: jax.ShapeDtypeStruct((b, s, w), BF16)
    tra = lambda w: jax.ShapeDtypeStruct((b, w, s), BF16)
    outs = [(tile(A_WIDTH), nat(A_WIDTH)), (tile(A_WIDTH), nat(A_WIDTH)), (tile(NA_WIDTH), nat(NA_WIDTH)),
            (tile(NA_WIDTH), nat(NA_WIDTH)), (tile_t(NA_VT_ROWS), tra(NA_VT_ROWS)),
            (tile(SW_Q_WIDTH), nat(SW_Q_WIDTH)), (tile(SW_KV_WIDTH), nat(SW_KV_WIDTH)),
            (tile_t(SW_VT_ROWS), tra(SW_VT_ROWS))]
    if kv_only:
        outs = [outs[3], outs[4], outs[6], outs[7]]
    return pl.pallas_call(
        functools.partial(_inproj_kernel, rope=rope, kv_only=kv_only),
        grid=(b // nb, s // tm),
        in_specs=[tile(d),
                  pl.BlockSpec((1, 6, d), lambda i, j: (i * nb, 0, 0)),
                  const(w_main.shape), const(w_vt.shape),
                  const((1, A_WIDTH)), const((1, A_WIDTH)),
                  pl.BlockSpec((tm, LANES), lambda i, j: (j, 0)),
                  pl.BlockSpec((tm, LANES), lambda i, j: (j, 0))],
        out_specs=[spec for spec, _ in outs],
        out_shape=[shape for _, shape in outs],
        compiler_params=_params(2),
        name="in_projection_rope" if rope else "in_projection",
    )(x, mod6, w_main, w_vt, ln_g, ln_b, cos, sin)


def _gmlp_chunk(u, v, ws_all, bsm):
    m = jnp.dot(ws_all, v, preferred_element_type=F32)
    low = lax.broadcasted_iota(jnp.int32, (1, LANES), 1) < HEAD_DIM
    mixed = jnp.concatenate(
        [jnp.where(low, m[g * CHUNK:(g + 1) * CHUNK, g // 2 * LANES:(g // 2 + 1) * LANES],
                   m[(g + 1) * CHUNK:(g + 2) * CHUNK, g // 2 * LANES:(g // 2 + 1) * LANES])
         for g in range(0, A_GROUPS, 2)], axis=1)
    return u.astype(F32) * (mixed + bsm)


def _stack_na_q(q):
    head = lax.broadcasted_iota(jnp.int32, (1, NA_WIDTH), 1) // HEAD_DIM
    zero = jnp.zeros_like(q)
    return jnp.concatenate([jnp.where(head == h, q, zero) for h in range(NA_HEADS)], axis=0)


def _stack_sw_q(q):
    low = lax.broadcasted_iota(jnp.int32, (1, LANES), 1) < HEAD_DIM
    slabs = [q[:, j * LANES:(j + 1) * LANES] for j in range(SW_GROUP)]
    zero = jnp.zeros_like(slabs[0])
    return jnp.concatenate([jnp.where(low, qj, zero) for qj in slabs]
                           + [jnp.where(low, zero, qj) for qj in slabs], axis=0)


def _exp_t(s_t, sink=None):
    m = jnp.max(s_t, axis=0, keepdims=True)
    if sink is not None:
        m = jnp.maximum(m, sink)
    return jnp.exp2((s_t - m).astype(BF16)), m


def _na_scores(q, kcat):
    return _nt_dot(kcat, _stack_na_q(q))


def _na_finish(e, v_t):
    n = e.shape[1] // NA_HEADS
    o_t = jnp.dot(v_t, e, preferred_element_type=F32)
    inv_l = 1.0 / o_t[NA_WIDTH:NA_WIDTH + 1]
    head = lax.broadcasted_iota(jnp.int32, (1, NA_WIDTH), 1) // HEAD_DIM
    if n != HEAD_DIM:
        o = (o_t[:NA_WIDTH] * inv_l).T
        out = o[0:n]
        for h in range(1, NA_HEADS):
            out = jnp.where(head == h, o[h * n:(h + 1) * n], out)
        return out
    groups = [slice((h // 2) * LANES, (h // 2 + 1) * LANES) for h in range(NA_HEADS)]
    z = jnp.concatenate([o_t[h * HEAD_DIM:(h + 1) * HEAD_DIM, g] * inv_l[:, g] for h, g in enumerate(groups)],
                        axis=0).T
    return jnp.where(head % 2 == 0, z[:n], z[n:])


def _sw_scores(q, kcat):
    return _nt_dot(kcat, _stack_sw_q(q))


def _sw_probs(s_t, sink):
    e, m = _exp_t(s_t, sink)
    return e, jnp.exp2(sink - m)


def _sw_finish(e, sink_e, v_t):
    n = e.shape[1] // SW_HEADS
    ones = v_t[SW_KV_WIDTH:]
    o_kv = []
    for kv in range(SW_KV_HEADS):
        cols = slice(kv * SW_GROUP * n, (kv + 1) * SW_GROUP * n)
        v_kv = jnp.concatenate([v_t[kv * HEAD_DIM:(kv + 1) * HEAD_DIM], ones], axis=0)
        o_t = jnp.dot(v_kv, e[:, cols], preferred_element_type=F32)
        inv_l = 1.0 / (o_t[HEAD_DIM:HEAD_DIM + 1] + sink_e[:, cols])
        o_kv.append(o_t[:HEAD_DIM] * inv_l)
    return [jnp.concatenate([o[:, j * n:(j + 1) * n] for o in o_kv], axis=0).T for j in range(SW_GROUP)]


def _mixer_kernel(u_ref, v_ref, naq_ref, nak_ref, navt_ref, swq_ref, swk_ref, swvt_ref,
                  cnak_ref, cnavt_ref, cswk_ref, cswvt_ref,
                  ws_ref, bsm_ref, bias_ref, swmask_ref, sink_ref,
                  y_ref, navt2_scr, *, tq, seq):
    tile = pl.program_id(1)
    rows_per_tile = tq // GRID_W
    n_rows = seq // GRID_W
    win_keys = NA_ROWS * GRID_W

    @pl.when(tile == 0)
    def _():
        navt2_scr[0] = navt_ref[0]
        navt2_scr[1, :, 0:seq - GRID_W] = navt_ref[0, :, GRID_W:seq]
        navt2_scr[1, :, seq - GRID_W:seq] = jnp.zeros((NA_VT_ROWS, GRID_W), BF16)

    ws_all = ws_ref[...]
    bsm = bsm_ref[...]

    def na_window(i):
        r = tile * rows_per_tile + i
        rs = jnp.clip(r - NA_ROWS // 2, 0, n_rows - NA_ROWS)
        return rs, r - rs

    def na_scores(i, _):
        rs, off = na_window(i)
        kstart = pl.multiple_of(rs * GRID_W, GRID_W)
        kcat = jnp.concatenate([nak_ref[0, pl.ds(kstart, win_keys), :], cnak_ref[0]], axis=0)
        s_t = _na_scores(naq_ref[0, i * GRID_W:(i + 1) * GRID_W, :], kcat)
        parts = [s_t[j * GRID_W:(j + 1) * GRID_W] + bias_ref[j + NA_ROWS - 1 - off] for j in range(NA_ROWS)]
        return jnp.concatenate(parts + [s_t[win_keys:]], axis=0)

    def na_probs(i, s_t):
        return _exp_t(s_t)[0]

    def na_finish(i, e):
        rs, _ = na_window(i)
        par = rs % 2
        vstart = pl.multiple_of((rs - par) * GRID_W, LANES)
        v_t = jnp.concatenate([navt2_scr[par, :, pl.ds(vstart, win_keys)], cnavt_ref[0]], axis=1)
        y_ref[0, i * GRID_W:(i + 1) * GRID_W, A_WIDTH:A_WIDTH + NA_WIDTH] = _na_finish(e, v_t).astype(BF16)

    n_blocks = seq // SW_BLOCK
    sink = sink_ref[...]

    def sw_starts(i):
        t0 = (tile * (tq // SW_BLOCK) + i) * SW_BLOCK
        return (pl.multiple_of(jnp.maximum(t0 - SW_BLOCK, 0), SW_BLOCK), pl.multiple_of(t0, SW_BLOCK),
                pl.multiple_of(jnp.minimum(t0 + SW_BLOCK, seq - SW_BLOCK), SW_BLOCK))

    def sw_scores(i, _):
        bi = tile * (tq // SW_BLOCK) + i
        kcat = jnp.concatenate([swk_ref[0, pl.ds(st, SW_BLOCK), :] for st in sw_starts(i)] + [cswk_ref[0]], axis=0)
        s_t = _sw_scores(swq_ref[0, i * SW_BLOCK:(i + 1) * SW_BLOCK, :], kcat)
        prev_mask = swmask_ref[jnp.where(bi > 0, 0, 2)]
        next_mask = swmask_ref[jnp.where(bi < n_blocks - 1, 1, 2)]
        return jnp.concatenate([s_t[0:SW_BLOCK] + prev_mask, s_t[SW_BLOCK:2 * SW_BLOCK],
                                s_t[2 * SW_BLOCK:3 * SW_BLOCK] + next_mask, s_t[3 * SW_BLOCK:]], axis=0)

    def sw_probs(i, s_t):
        return _sw_probs(s_t, sink)

    def sw_finish(i, probs):
        v_t = jnp.concatenate([swvt_ref[0, :, pl.ds(st, SW_BLOCK)] for st in sw_starts(i)] + [cswvt_ref[0]], axis=1)
        base = A_WIDTH + NA_WIDTH
        for j, slab in enumerate(_sw_finish(*probs, v_t)):
            y_ref[0, i * SW_BLOCK:(i + 1) * SW_BLOCK, base + j * LANES:base + (j + 1) * LANES] = slab.astype(BF16)

    for c in range(tq // CHUNK):
        sl = slice(c * CHUNK, (c + 1) * CHUNK)
        y_ref[0, sl, 0:A_WIDTH] = _gmlp_chunk(u_ref[0, sl, :], v_ref[0, sl, :], ws_all, bsm).astype(BF16)
    fns = {"na": [na_scores, na_probs, na_finish], "sw": [sw_scores, sw_probs, sw_finish]}
    items = []
    for c in range(tq // SW_BLOCK):
        items += [("na", 2 * c), ("na", 2 * c + 1), ("sw", c)]
    _pipelined(len(items), [lambda t, v, k=k: fns[items[t][0]][k](items[t][1], v) for k in range(3)])


def _mixer(u, v, naq, nak, navt, swq, swk, swvt, cnak, cnavt, cswk, cswvt,
           ws_all, bsm, bias_tab, sw_mask, sink, *, tq):
    b, s, _ = u.shape
    d = A_WIDTH + NA_WIDTH + SW_Q_WIDTH
    lc = cnak.shape[1]
    tile = lambda w: pl.BlockSpec((1, tq, w), lambda i, j: (i, j, 0))
    full = lambda n, w: pl.BlockSpec((1, n, w), lambda i, j: (i, 0, 0))
    const = lambda shape: pl.BlockSpec(shape, lambda i, j: (0,) * len(shape))
    return pl.pallas_call(
        functools.partial(_mixer_kernel, tq=tq, seq=s),
        grid=(b, s // tq),
        in_specs=[tile(A_WIDTH), tile(A_WIDTH),
                  tile(NA_WIDTH), full(s, NA_WIDTH), full(NA_VT_ROWS, s),
                  tile(SW_Q_WIDTH), full(s, SW_KV_WIDTH), full(SW_VT_ROWS, s),
                  full(lc, NA_WIDTH), full(NA_VT_ROWS, lc), full(lc, SW_KV_WIDTH), full(SW_VT_ROWS, lc),
                  const(ws_all.shape), const(bsm.shape), const(bias_tab.shape), const(sw_mask.shape),
                  const(sink.shape)],
        out_specs=tile(d),
        out_shape=jax.ShapeDtypeStruct((b, s, d), BF16),
        scratch_shapes=[pltpu.VMEM((2, NA_VT_ROWS, s), BF16)],
        compiler_params=_params(2),
        name="mixer",
    )(u, v, naq, nak, navt, swq, swk, swvt, cnak, cnavt, cswk, cswvt,
      ws_all, bsm, bias_tab, sw_mask, sink)


def _ctx_mixer_kernel(u_ref, v_ref, naq_ref, nak_ref, navt_ref, swq_ref, swk_ref, swvt_ref,
                      ws_ref, bsm_ref, sink_ref, y_ref, *, lc):
    ws_all = ws_ref[...]
    bsm = bsm_ref[...]
    for c in range(lc // CHUNK):
        sl = slice(c * CHUNK, (c + 1) * CHUNK)
        y_ref[0, sl, 0:A_WIDTH] = _gmlp_chunk(u_ref[0, sl, :], v_ref[0, sl, :], ws_all, bsm).astype(BF16)

    out = _na_finish(_exp_t(_na_scores(naq_ref[0], nak_ref[0]))[0], navt_ref[0])
    y_ref[0, :, A_WIDTH:A_WIDTH + NA_WIDTH] = out.astype(BF16)

    slabs = _sw_finish(*_sw_probs(_sw_scores(swq_ref[0], swk_ref[0]), sink_ref[...]), swvt_ref[0])
    base = A_WIDTH + NA_WIDTH
    for j, slab in enumerate(slabs):
        y_ref[0, :, base + j * LANES:base + (j + 1) * LANES] = slab.astype(BF16)


def _ctx_mixer(u, v, naq, nak, navt, swq, swk, swvt, ws_all, bsm, sink):
    b, lc, _ = u.shape
    d = A_WIDTH + NA_WIDTH + SW_Q_WIDTH
    full = lambda n, w: pl.BlockSpec((1, n, w), lambda i: (i, 0, 0))
    const = lambda shape: pl.BlockSpec(shape, lambda i: (0,) * len(shape))
    return pl.pallas_call(
        functools.partial(_ctx_mixer_kernel, lc=lc),
        grid=(b,),
        in_specs=[full(lc, A_WIDTH), full(lc, A_WIDTH), full(lc, NA_WIDTH), full(lc, NA_WIDTH),
                  full(NA_VT_ROWS, lc), full(lc, SW_Q_WIDTH), full(lc, SW_KV_WIDTH), full(SW_VT_ROWS, lc),
                  const(ws_all.shape), const(bsm.shape), const(sink.shape)],
        out_specs=full(lc, d),
        out_shape=jax.ShapeDtypeStruct((b, lc, d), BF16),
        compiler_params=_params(1),
        name="ctx_mixer",
    )(u, v, naq, nak, navt, swq, swk, swvt, ws_all, bsm, sink)


def _out_ffn_kernel(y_ref, x_ref, mod_ref, wout_ref, l1g_ref, l1b_ref, w1_ref, w2_ref, l2g_ref, l2b_ref,
                    o_ref, *, alpha):
    tm = x_ref.shape[1]
    rows = [slice(i * FFN_SUB, (i + 1) * FFN_SUB) for i in range(tm // FFN_SUB)]

    def out_proj(i, _):
        return jnp.dot(y_ref[0, rows[i], :], wout_ref[...], preferred_element_type=F32)

    def norm1(i, yo):
        x = _layer_norm(alpha * x_ref[0, rows[i], :] + mod_ref[0, 2:3, :] * yo, l1g_ref[...], l1b_ref[...])
        return x, (x * (1.0 + mod_ref[0, 4:5, :]) + mod_ref[0, 3:4, :]).astype(BF16)

    def up(i, xh):
        return xh[0], jnp.dot(xh[1], w1_ref[...], preferred_element_type=F32)

    def down(i, xa):
        a = jnp.square(jnp.maximum(xa[1], 0.0)).astype(BF16)
        return xa[0], jnp.dot(a, w2_ref[...], preferred_element_type=F32)

    def norm2(i, xf):
        o_ref[0, rows[i], :] = _layer_norm(alpha * xf[0] + mod_ref[0, 5:6, :] * xf[1], l2g_ref[...], l2b_ref[...])

    _pipelined(len(rows), [out_proj, norm1, up, down, norm2])


def _out_ffn(y, x, mod6, wout, l1g, l1b, w1, w2, l2g, l2b, *, tm, alpha):
    b, s, d = x.shape
    tile = pl.BlockSpec((1, tm, d), lambda i, j: (i, j, 0))
    const = lambda shape: pl.BlockSpec(shape, lambda i, j: (0,) * len(shape),
                                       pipeline_mode=pl.Buffered(1))
    return pl.pallas_call(
        functools.partial(_out_ffn_kernel, alpha=alpha),
        grid=(b, s // tm),
        in_specs=[tile, tile, pl.BlockSpec((1, 6, d), lambda i, j: (i, 0, 0)),
                  const(wout.shape), const((1, d)), const((1, d)),
                  const(w1.shape), const(w2.shape), const((1, d)), const((1, d))],
        out_specs=tile,
        out_shape=jax.ShapeDtypeStruct((b, s, d), F32),
        compiler_params=_params(2),
        name="out_ffn",
    )(y, x, mod6, wout, l1g, l1b, w1, w2, l2g, l2b)


def _sw_head_order():
    return [kv * SW_GROUP + j for j in range(SW_GROUP) for kv in range(SW_KV_HEADS)]


def _rope_tables(seq):
    pos = jnp.arange(seq)
    row_pos, col_pos = pos // GRID_W, pos % GRID_W
    inv_freq = ROPE_BASE ** (-jnp.arange(ROPE_FREQS, dtype=F32) / ROPE_FREQS)
    ang_r = row_pos.astype(F32)[:, None] * inv_freq
    ang_c = col_pos.astype(F32)[:, None] * inv_freq
    cos_head = jnp.concatenate([jnp.cos(ang_r), jnp.cos(ang_r), jnp.cos(ang_c), jnp.cos(ang_c)], axis=-1)
    sin_head = jnp.concatenate([-jnp.sin(ang_r), jnp.sin(ang_r), -jnp.sin(ang_c), jnp.sin(ang_c)], axis=-1)
    return jnp.tile(cos_head, (1, LANES // HEAD_DIM)), jnp.tile(sin_head, (1, LANES // HEAD_DIM))


def _na_bias_table(rpb):
    cq = np.arange(GRID_W)
    cs = np.clip(cq - NA_COLS // 2, 0, GRID_W - NA_COLS)
    col_ok = (cq[None, :] >= cs[:, None]) & (cq[None, :] < cs[:, None] + NA_COLS)
    ext = GRID_W - NA_COLS
    rpb_ext = jnp.concatenate([jnp.repeat(rpb[..., :1], ext, axis=-1), rpb,
                               jnp.repeat(rpb[..., -1:], ext, axis=-1)], axis=-1)
    rel = jnp.stack([rpb_ext[..., GRID_W - 1 - q:2 * GRID_W - 1 - q] for q in range(GRID_W)], axis=2)
    bias_col = jnp.where(col_ok[None, None], rel * LOG2E, NEG_INF)
    return bias_col.transpose(1, 3, 0, 2).reshape(2 * NA_ROWS - 1, GRID_W, NA_HEADS * GRID_W)


def _sw_mask_table():
    kj = np.arange(SW_BLOCK)[:, None]
    qi = np.arange(SW_BLOCK)[None, :]
    prev = np.where(kj >= qi, 0.0, NEG_INF)
    nxt = np.where(kj <= qi, 0.0, NEG_INF)
    none = np.full((SW_BLOCK, SW_BLOCK), NEG_INF)
    return jnp.asarray(np.stack([np.tile(m, (1, SW_HEADS)) for m in (prev, nxt, none)]), F32)


def kernel(x, c, ctx, c_ctx, w_mod, b_mod, w_in, a_ln_g, a_ln_b, a_ws, a_bs, na_rpb, sw_sink, w_out,
           ln1_g, ln1_b, w1, w2, ln2_g, ln2_b):
    depth = w_mod.shape[0]
    b, s, d = x.shape
    lc = ctx.shape[1]
    alpha = (2 * depth) ** 0.25
    tq = 512
    tm = 1024
    assert s % tm == 0 and tm % tq == 0 and (b * lc) % tm == 0 and lc % CHUNK == 0
    ctx_nb = math.gcd(b, max(1, tm // lc))

    rows = -(-(b + 1) // MOD_ROWS_PAD) * MOD_ROWS_PAD
    cc = jnp.concatenate([c, c_ctx[None], jnp.zeros((rows - b - 1, d), F32)], axis=0)
    mod_all = _modulation(cc, w_mod, b_mod)

    q0 = 2 * A_WIDTH + 3 * NA_WIDTH
    nav0 = 2 * A_WIDTH + 2 * NA_WIDTH
    k0 = q0 + SW_Q_WIDTH
    heads = _sw_head_order()
    cos, sin = _rope_tables(s)
    sw_mask = _sw_mask_table()

    xc = ctx
    for layer in range(depth):
        mod6 = mod_all[layer, :b].reshape(b, 6, d)
        modc6 = jnp.broadcast_to(mod_all[layer, b].reshape(1, 6, d), (b, 6, d))
        wl = w_in[layer]
        w_main = jnp.concatenate(
            [wl[:, :nav0]] + [wl[:, q0 + h * HEAD_DIM:q0 + (h + 1) * HEAD_DIM] for h in heads]
            + [wl[:, k0:k0 + SW_KV_WIDTH]], axis=1).astype(BF16)
        w_vt = jnp.concatenate([wl[:, nav0:q0], wl[:, k0 + SW_KV_WIDTH:]], axis=1).T.astype(BF16)
        wo = w_out[layer]
        y0 = A_WIDTH + NA_WIDTH
        w_out_l = jnp.concatenate(
            [wo[:y0]] + [wo[y0 + h * HEAD_DIM:y0 + (h + 1) * HEAD_DIM] for h in heads], axis=0).astype(BF16)
        w1_l = w1[layer].astype(BF16)
        w2_l = w2[layer].astype(BF16)
        lng = a_ln_g[layer].reshape(1, A_WIDTH)
        lnb = a_ln_b[layer].reshape(1, A_WIDTH)
        ws_all = a_ws[layer].reshape(A_GROUPS * CHUNK, CHUNK).astype(BF16)
        bsm = jnp.repeat(a_bs[layer].T, HEAD_DIM, axis=1)
        bias_tab = _na_bias_table(na_rpb[layer])
        sink8 = sw_sink[layer] * LOG2E
        sink_blk = jnp.repeat(sink8, SW_BLOCK)[None, :]
        sink_ctx = jnp.repeat(sink8, lc)[None, :]
        l1g, l1b = ln1_g[layer].reshape(1, d), ln1_b[layer].reshape(1, d)
        l2g, l2b = ln2_g[layer].reshape(1, d), ln2_b[layer].reshape(1, d)

        u, v, naq, nak, navt, swq, swk, swvt = _in_projection(
            x, mod6, w_main, w_vt, lng, lnb, cos, sin, rope=True, tm=tm)
        last = layer == depth - 1
        ctx_proj = _in_projection(xc, modc6, w_main, w_vt, lng, lnb, cos, sin, rope=False, tm=lc,
                                  nb=ctx_nb, kv_only=last)
        if last:
            cnak, cnavt, cswk, cswvt = ctx_proj
        else:
            cu, cv, cnaq, cnak, cnavt, cswq, cswk, cswvt = ctx_proj

        y = _mixer(u, v, naq, nak, navt, swq, swk, swvt, cnak, cnavt, cswk, cswvt,
                   ws_all, bsm, bias_tab, sw_mask, sink_blk, tq=tq)
        x = _out_ffn(y, x, mod6, w_out_l, l1g, l1b, w1_l, w2_l, l2g, l2b, tm=tm, alpha=alpha)

        if not last:
            yc = _ctx_mixer(cu, cv, cnaq, cnak, cnavt, cswq, cswk, cswvt, ws_all, bsm, sink_ctx)
            groups = b * lc // tm
            xc = _out_ffn(yc.reshape(groups, tm, d), xc.reshape(groups, tm, d), modc6[:groups], w_out_l, l1g, l1b,
                          w1_l, w2_l, l2g, l2b, tm=tm, alpha=alpha).reshape(b, lc, d)
    return x
```

```python
import functools
import math

import numpy as np
import jax
import jax.numpy as jnp
from jax import lax
from jax.experimental import pallas as pl
from jax.experimental.pallas import tpu as pltpu

F32 = jnp.float32
BF16 = jnp.bfloat16

D_MODEL = 1024
HEAD_DIM = 64
GRID_W = 64
CHUNK = 128
A_GROUPS = 4
A_WIDTH = 256
NA_HEADS = 4
NA_WIDTH = 256
NA_ROWS = 8
NA_COLS = 16
SW_KV_HEADS = 2
SW_GROUP = 4
SW_HEADS = SW_KV_HEADS * SW_GROUP
SW_Q_WIDTH = 512
SW_KV_WIDTH = 128
SW_BLOCK = 128
IN_WIDTH = 2048
VT_WIDTH = NA_WIDTH + SW_KV_WIDTH
ONES_ROWS = 16
NA_VT_ROWS = NA_WIDTH + ONES_ROWS
SW_VT_ROWS = SW_KV_WIDTH + ONES_ROWS
ROPE_BASE = 10000.0
ROPE_FREQS = 16
LN_EPS = 1e-5
NEG_INF = -1e30
LOG2E = math.log2(math.e)
Q_SCALE = HEAD_DIM ** -0.5 * LOG2E

FFN_SUB = 256
LANES = 128
MOD_ROWS_PAD = 8
VMEM_LIMIT = 56 * 1024 * 1024


def _layer_norm(x, g, b):
    mu = jnp.mean(x, axis=-1, keepdims=True)
    var = jnp.mean(jnp.square(x - mu), axis=-1, keepdims=True)
    return (x - mu) * lax.rsqrt(var + LN_EPS) * g + b


def _params(n_grid):
    return pltpu.CompilerParams(dimension_semantics=("arbitrary",) * n_grid,
                                vmem_limit_bytes=VMEM_LIMIT)


def _pipelined(n, stages):
    vals = {}
    for t in range(n + len(stages) - 1):
        for k, stage in reversed(list(enumerate(stages))):
            if 0 <= t - k < n:
                vals[k, t - k] = stage(t - k, vals.pop((k - 1, t - k), None))


def _nt_dot(a, b):
    return lax.dot_general(a, b, (((1,), (1,)), ((), ())), preferred_element_type=F32)


def _mod_kernel(c_ref, w_ref, b_ref, o_ref):
    a = jax.nn.silu(c_ref[...]).astype(BF16)
    o_ref[0] = jnp.dot(a, w_ref[0].astype(BF16), preferred_element_type=F32) + b_ref[0]


def _modulation(cc, w_mod, b_mod, tn=1024):
    depth, d, n = w_mod.shape
    rows = cc.shape[0]
    return pl.pallas_call(
        _mod_kernel,
        grid=(depth, n // tn),
        in_specs=[pl.BlockSpec((rows, d), lambda l, j: (0, 0)),
                  pl.BlockSpec((1, d, tn), lambda l, j: (l, 0, j)),
                  pl.BlockSpec((1, 1, tn), lambda l, j: (l, 0, j))],
        out_specs=pl.BlockSpec((1, rows, tn), lambda l, j: (l, 0, j)),
        out_shape=jax.ShapeDtypeStruct((depth, rows, n), F32),
        compiler_params=_params(2),
        name="modulation",
    )(cc, w_mod, b_mod.reshape(depth, 1, n))


def _inproj_kernel(x_ref, mod_ref, w_ref, wvt_ref, lng_ref, lnb_ref, cos_ref, sin_ref,
                   *out_refs, rope, kv_only):
    if kv_only:
        nak_ref, navt_ref, swk_ref, swvt_ref = out_refs
    else:
        u_ref, v_ref, naq_ref, nak_ref, navt_ref, swq_ref, swk_ref, swvt_ref = out_refs
    nb, tm, d = x_ref.shape
    x = x_ref[...].reshape(nb * tm, d)
    h = (x * (1.0 + mod_ref[0, 1:2, :]) + mod_ref[0, 0:1, :]).astype(BF16)

    def put(ref, val):
        ref[...] = val.astype(BF16).reshape(ref.shape)

    def proj(lo, hi):
        return jnp.dot(h, w_ref[:, lo:hi], preferred_element_type=F32)

    if rope:
        cos = cos_ref[...]
        sin = sin_ref[...]
        first = (lax.broadcasted_iota(jnp.int32, (1, LANES), 1) % 32) < 16

        def rot(t):
            partner = jnp.where(first, pltpu.roll(t, LANES - 16, axis=1), pltpu.roll(t, 16, axis=1))
            return t * cos + partner * sin
    else:
        def rot(t):
            return t

    if not kv_only:
        put(u_ref, jax.nn.gelu(proj(0, 256)))
        put(v_ref, _layer_norm(jax.nn.gelu(proj(256, 512)), lng_ref[...], lnb_ref[...]))
        put(naq_ref, proj(512, 768) * Q_SCALE)
    put(nak_ref, proj(768, 1024))
    if kv_only:
        put(swk_ref, rot(proj(1536, 1664)))
    else:
        qk = proj(1024, 1664)
        for j in range(SW_Q_WIDTH // LANES):
            sl = slice(j * LANES, (j + 1) * LANES)
            put(swq_ref.at[:, :, sl], rot(qk[:, sl]) * Q_SCALE)
        put(swk_ref, rot(qk[:, SW_Q_WIDTH:]))
    vt = _nt_dot(wvt_ref[...], h).astype(BF16)
    ones = jnp.ones((ONES_ROWS, tm), BF16)
    for i in range(nb):
        cols = slice(i * tm, (i + 1) * tm)
        navt_ref[i, :NA_WIDTH, :] = vt[:NA_WIDTH, cols]
        navt_ref[i, NA_WIDTH:, :] = ones
        swvt_ref[i, :SW_KV_WIDTH, :] = vt[NA_WIDTH:, cols]
        swvt_ref[i, SW_KV_WIDTH:, :] = ones


def _in_projection(x, mod6, w_main, w_vt, ln_g, ln_b, cos, sin, *, rope, tm, nb=1, kv_only=False):
    b, s, d = x.shape
    assert b % nb == 0 and not (rope and nb > 1)
    tile = lambda w: pl.BlockSpec((nb, tm, w), lambda i, j: (i, j, 0))
    tile_t = lambda w: pl.BlockSpec((nb, w, tm), lambda i, j: (i, 0, j))
    const = lambda shape: pl.BlockSpec(shape, lambda i, j: (0,) * len(shape))
    nat = lambda w: jax.ShapeDtypeStruct((b, s, w), BF16)
    tra = lambda w: jax.ShapeDtypeStruct((b, w, s), BF16)
    outs = [(tile(A_WIDTH), nat(A_WIDTH)), (tile(A_WIDTH), nat(A_WIDTH)), (tile(NA_WIDTH), nat(NA_WIDTH)),
            (tile(NA_WIDTH), nat(NA_WIDTH)), (tile_t(NA_VT_ROWS), tra(NA_VT_ROWS)),
            (tile(SW_Q_WIDTH), nat(SW_Q_WIDTH)), (tile(SW_KV_WIDTH), nat(SW_KV_WIDTH)),
            (tile_t(SW_VT_ROWS), tra(SW_VT_ROWS))]
    if kv_only:
        outs = [outs[3], outs[4], outs[6], outs[7]]
    return pl.pallas_call(
        functools.partial(_inproj_kernel, rope=rope, kv_only=kv_only),
        grid=(b // nb, s // tm),
        in_specs=[tile(d),
                  pl.BlockSpec((1, 6, d), lambda i, j: (i * nb, 0, 0)),
                  const(w_main.shape), const(w_vt.shape),
                  const((1, A_WIDTH)), const((1, A_WIDTH)),
                  pl.BlockSpec((tm, LANES), lambda i, j: (j, 0)),
                  pl.BlockSpec((tm, LANES), lambda i, j: (j, 0))],
        out_specs=[spec for spec, _ in outs],
        out_shape=[shape for _, shape in outs],
        compiler_params=_params(2),
        name="in_projection_rope" if rope else "in_projection",
    )(x, mod6, w_main, w_vt, ln_g, ln_b, cos, sin)


def _gmlp_chunk(u, v, ws_all, bsm):
    m = jnp.dot(ws_all, v, preferred_element_type=F32)
    low = lax.broadcasted_iota(jnp.int32, (1, LANES), 1) < HEAD_DIM
    mixed = jnp.concatenate(
        [jnp.where(low, m[g * CHUNK:(g + 1) * CHUNK, g // 2 * LANES:(g // 2 + 1) * LANES],
                   m[(g + 1) * CHUNK:(g + 2) * CHUNK, g // 2 * LANES:(g // 2 + 1) * LANES])
         for g in range(0, A_GROUPS, 2)], axis=1)
    return u.astype(F32) * (mixed + bsm)


def _stack_na_q(q):
    head = lax.broadcasted_iota(jnp.int32, (1, NA_WIDTH), 1) // HEAD_DIM
    zero = jnp.zeros_like(q)
    return jnp.concatenate([jnp.where(head == h, q, zero) for h in range(NA_HEADS)], axis=0)


def _stack_sw_q(q):
    low = lax.broadcasted_iota(jnp.int32, (1, LANES), 1) < HEAD_DIM
    slabs = [q[:, j * LANES:(j + 1) * LANES] for j in range(SW_GROUP)]
    zero = jnp.zeros_like(slabs[0])
    return jnp.concatenate([jnp.where(low, qj, zero) for qj in slabs]
                           + [jnp.where(low, zero, qj) for qj in slabs], axis=0)


def _exp_t(s_t, sink=None):
    m = jnp.max(s_t, axis=0, keepdims=True)
    if sink is not None:
        m = jnp.maximum(m, sink)
    return jnp.exp2((s_t - m).astype(BF16)), m


def _na_scores(q, kcat):
    return _nt_dot(kcat, _stack_na_q(q))


def _na_finish(e, v_t):
    n = e.shape[1] // NA_HEADS
    o_t = jnp.dot(v_t, e, preferred_element_type=F32)
    inv_l = 1.0 / o_t[NA_WIDTH:NA_WIDTH + 1]
    head = lax.broadcasted_iota(jnp.int32, (1, NA_WIDTH), 1) // HEAD_DIM
    if n != HEAD_DIM:
        o = (o_t[:NA_WIDTH] * inv_l).T
        out = o[0:n]
        for h in range(1, NA_HEADS):
            out = jnp.where(head == h, o[h * n:(h + 1) * n], out)
        return out
    groups = [slice((h // 2) * LANES, (h // 2 + 1) * LANES) for h in range(NA_HEADS)]
    z = jnp.concatenate([o_t[h * HEAD_DIM:(h + 1) * HEAD_DIM, g] * inv_l[:, g] for h, g in enumerate(groups)],
                        axis=0).T
    return jnp.where(head % 2 == 0, z[:n], z[n:])


def _sw_scores(q, kcat):
    return _nt_dot(kcat, _stack_sw_q(q))


def _sw_probs(s_t, sink):
    e, m = _exp_t(s_t, sink)
    return e, jnp.exp2(sink - m)


def _sw_finish(e, sink_e, v_t):
    n = e.shape[1] // SW_HEADS
    ones = v_t[SW_KV_WIDTH:]
    o_kv = []
    for kv in range(SW_KV_HEADS):
        cols = slice(kv * SW_GROUP * n, (kv + 1) * SW_GROUP * n)
        v_kv = jnp.concatenate([v_t[kv * HEAD_DIM:(kv + 1) * HEAD_DIM], ones], axis=0)
        o_t = jnp.dot(v_kv, e[:, cols], preferred_element_type=F32)
        inv_l = 1.0 / (o_t[HEAD_DIM:HEAD_DIM + 1] + sink_e[:, cols])
        o_kv.append(o_t[:HEAD_DIM] * inv_l)
    return [jnp.concatenate([o[:, j * n:(j + 1) * n] for o in o_kv], axis=0).T for j in range(SW_GROUP)]


def _mixer_kernel(u_ref, v_ref, naq_ref, nak_ref, navt_ref, swq_ref, swk_ref, swvt_ref,
                  cnak_ref, cnavt_ref, cswk_ref, cswvt_ref,
                  ws_ref, bsm_ref, bias_ref, swmask_ref, sink_ref,
                  y_ref, navt2_scr, *, tq, seq):
    tile = pl.program_id(1)
    rows_per_tile = tq // GRID_W
    n_rows = seq // GRID_W
    win_keys = NA_ROWS * GRID_W

    @pl.when(tile == 0)
    def _():
        navt2_scr[0] = navt_ref[0]
        navt2_scr[1, :, 0:seq - GRID_W] = navt_ref[0, :, GRID_W:seq]
        navt2_scr[1, :, seq - GRID_W:seq] = jnp.zeros((NA_VT_ROWS, GRID_W), BF16)

    ws_all = ws_ref[...]
    bsm = bsm_ref[...]

    def na_window(i):
        r = tile * rows_per_tile + i
        rs = jnp.clip(r - NA_ROWS // 2, 0, n_rows - NA_ROWS)
        return rs, r - rs

    def na_scores(i, _):
        rs, off = na_window(i)
        kstart = pl.multiple_of(rs * GRID_W, GRID_W)
        kcat = jnp.concatenate([nak_ref[0, pl.ds(kstart, win_keys), :], cnak_ref[0]], axis=0)
        s_t = _na_scores(naq_ref[0, i * GRID_W:(i + 1) * GRID_W, :], kcat)
        parts = [s_t[j * GRID_W:(j + 1) * GRID_W] + bias_ref[j + NA_ROWS - 1 - off] for j in range(NA_ROWS)]
        return jnp.concatenate(parts + [s_t[win_keys:]], axis=0)

    def na_probs(i, s_t):
        return _exp_t(s_t)[0]

    def na_finish(i, e):
        rs, _ = na_window(i)
        par = rs % 2
        vstart = pl.multiple_of((rs - par) * GRID_W, LANES)
        v_t = jnp.concatenate([navt2_scr[par, :, pl.ds(vstart, win_keys)], cnavt_ref[0]], axis=1)
        y_ref[0, i * GRID_W:(i + 1) * GRID_W, A_WIDTH:A_WIDTH + NA_WIDTH] = _na_finish(e, v_t).astype(BF16)

    n_blocks = seq // SW_BLOCK
    sink = sink_ref[...]

    def sw_starts(i):
        t0 = (tile * (tq // SW_BLOCK) + i) * SW_BLOCK
        return (pl.multiple_of(jnp.maximum(t0 - SW_BLOCK, 0), SW_BLOCK), pl.multiple_of(t0, SW_BLOCK),
                pl.multiple_of(jnp.minimum(t0 + SW_BLOCK, seq - SW_BLOCK), SW_BLOCK))

    def sw_scores(i, _):
        bi = tile * (tq // SW_BLOCK) + i
        kcat = jnp.concatenate([swk_ref[0, pl.ds(st, SW_BLOCK), :] for st in sw_starts(i)] + [cswk_ref[0]], axis=0)
        s_t = _sw_scores(swq_ref[0, i * SW_BLOCK:(i + 1) * SW_BLOCK, :], kcat)
        prev_mask = swmask_ref[jnp.where(bi > 0, 0, 2)]
        next_mask = swmask_ref[jnp.where(bi < n_blocks - 1, 1, 2)]
        return jnp.concatenate([s_t[0:SW_BLOCK] + prev_mask, s_t[SW_BLOCK:2 * SW_BLOCK],
                                s_t[2 * SW_BLOCK:3 * SW_BLOCK] + next_mask, s_t[3 * SW_BLOCK:]], axis=0)

    def sw_probs(i, s_t):
        return _sw_probs(s_t, sink)

    def sw_finish(i, probs):
        v_t = jnp.concatenate([swvt_ref[0, :, pl.ds(st, SW_BLOCK)] for st in sw_starts(i)] + [cswvt_ref[0]], axis=1)
        base = A_WIDTH + NA_WIDTH
        for j, slab in enumerate(_sw_finish(*probs, v_t)):
            y_ref[0, i * SW_BLOCK:(i + 1) * SW_BLOCK, base + j * LANES:base + (j + 1) * LANES] = slab.astype(BF16)

    for c in range(tq // CHUNK):
        sl = slice(c * CHUNK, (c + 1) * CHUNK)
        y_ref[0, sl, 0:A_WIDTH] = _gmlp_chunk(u_ref[0, sl, :], v_ref[0, sl, :], ws_all, bsm).astype(BF16)
    fns = {"na": [na_scores, na_probs, na_finish], "sw": [sw_scores, sw_probs, sw_finish]}
    items = []
    for c in range(tq // SW_BLOCK):
        items += [("na", 2 * c), ("na", 2 * c + 1), ("sw", c)]
    _pipelined(len(items), [lambda t, v, k=k: fns[items[t][0]][k](items[t][1], v) for k in range(3)])


def _mixer(u, v, naq, nak, navt, swq, swk, swvt, cnak, cnavt, cswk, cswvt,
           ws_all, bsm, bias_tab, sw_mask, sink, *, tq):
    b, s, _ = u.shape
    d = A_WIDTH + NA_WIDTH + SW_Q_WIDTH
    lc = cnak.shape[1]
    tile = lambda w: pl.BlockSpec((1, tq, w), lambda i, j: (i, j, 0))
    full = lambda n, w: pl.BlockSpec((1, n, w), lambda i, j: (i, 0, 0))
    const = lambda shape: pl.BlockSpec(shape, lambda i, j: (0,) * len(shape))
    return pl.pallas_call(
        functools.partial(_mixer_kernel, tq=tq, seq=s),
        grid=(b, s // tq),
        in_specs=[tile(A_WIDTH), tile(A_WIDTH),
                  tile(NA_WIDTH), full(s, NA_WIDTH), full(NA_VT_ROWS, s),
                  tile(SW_Q_WIDTH), full(s, SW_KV_WIDTH), full(SW_VT_ROWS, s),
                  full(lc, NA_WIDTH), full(NA_VT_ROWS, lc), full(lc, SW_KV_WIDTH), full(SW_VT_ROWS, lc),
                  const(ws_all.shape), const(bsm.shape), const(bias_tab.shape), const(sw_mask.shape),
                  const(sink.shape)],
        out_specs=tile(d),
        out_shape=jax.ShapeDtypeStruct((b, s, d), BF16),
        scratch_shapes=[pltpu.VMEM((2, NA_VT_ROWS, s), BF16)],
        compiler_params=_params(2),
        name="mixer",
    )(u, v, naq, nak, navt, swq, swk, swvt, cnak, cnavt, cswk, cswvt,
      ws_all, bsm, bias_tab, sw_mask, sink)


def _ctx_mixer_kernel(u_ref, v_ref, naq_ref, nak_ref, navt_ref, swq_ref, swk_ref, swvt_ref,
                      ws_ref, bsm_ref, sink_ref, y_ref, *, lc):
    ws_all = ws_ref[...]
    bsm = bsm_ref[...]
    for c in range(lc // CHUNK):
        sl = slice(c * CHUNK, (c + 1) * CHUNK)
        y_ref[0, sl, 0:A_WIDTH] = _gmlp_chunk(u_ref[0, sl, :], v_ref[0, sl, :], ws_all, bsm).astype(BF16)

    def na_finish(e):
        y_ref[0, :, A_WIDTH:A_WIDTH + NA_WIDTH] = _na_finish(e, navt_ref[0]).astype(BF16)

    def sw_finish(probs):
        base = A_WIDTH + NA_WIDTH
        for j, slab in enumerate(_sw_finish(*probs, swvt_ref[0])):
            y_ref[0, :, base + j * LANES:base + (j + 1) * LANES] = slab.astype(BF16)

    stages = [[lambda _: _na_scores(naq_ref[0], nak_ref[0]), lambda s_t: _exp_t(s_t)[0], na_finish],
              [lambda _: _sw_scores(swq_ref[0], swk_ref[0]), lambda s_t: _sw_probs(s_t, sink_ref[...]), sw_finish]]
    _pipelined(2, [lambda i, v, k=k: stages[i][k](v) for k in range(3)])


def _ctx_mixer(u, v, naq, nak, navt, swq, swk, swvt, ws_all, bsm, sink):
    b, lc, _ = u.shape
    d = A_WIDTH + NA_WIDTH + SW_Q_WIDTH
    full = lambda n, w: pl.BlockSpec((1, n, w), lambda i: (i, 0, 0))
    const = lambda shape: pl.BlockSpec(shape, lambda i: (0,) * len(shape))
    return pl.pallas_call(
        functools.partial(_ctx_mixer_kernel, lc=lc),
        grid=(b,),
        in_specs=[full(lc, A_WIDTH), full(lc, A_WIDTH), full(lc, NA_WIDTH), full(lc, NA_WIDTH),
                  full(NA_VT_ROWS, lc), full(lc, SW_Q_WIDTH), full(lc, SW_KV_WIDTH), full(SW_VT_ROWS, lc),
                  const(ws_all.shape), const(bsm.shape), const(sink.shape)],
        out_specs=full(lc, d),
        out_shape=jax.ShapeDtypeStruct((b, lc, d), BF16),
        compiler_params=_params(1),
        name="ctx_mixer",
    )(u, v, naq, nak, navt, swq, swk, swvt, ws_all, bsm, sink)


def _out_ffn_kernel(y_ref, x_ref, mod_ref, wout_ref, l1g_ref, l1b_ref, w1_ref, w2_ref, l2g_ref, l2b_ref,
                    o_ref, *, alpha):
    tm = x_ref.shape[1]
    rows = [slice(i * FFN_SUB, (i + 1) * FFN_SUB) for i in range(tm // FFN_SUB)]

    def out_proj(i, _):
        return jnp.dot(y_ref[0, rows[i], :], wout_ref[...], preferred_element_type=F32)

    def norm1(i, yo):
        x = _layer_norm(alpha * x_ref[0, rows[i], :] + mod_ref[0, 2:3, :] * yo, l1g_ref[...], l1b_ref[...])
        return x, (x * (1.0 + mod_ref[0, 4:5, :]) + mod_ref[0, 3:4, :]).astype(BF16)

    def up(i, xh):
        return xh[0], jnp.dot(xh[1], w1_ref[...], preferred_element_type=F32)

    def down(i, xa):
        a = jnp.square(jnp.maximum(xa[1], 0.0)).astype(BF16)
        return xa[0], jnp.dot(a, w2_ref[...], preferred_element_type=F32)

    def norm2(i, xf):
        o_ref[0, rows[i], :] = _layer_norm(alpha * xf[0] + mod_ref[0, 5:6, :] * xf[1], l2g_ref[...], l2b_ref[...])

    _pipelined(len(rows), [out_proj, norm1, up, down, norm2])


def _out_ffn(y, x, mod6, wout, l1g, l1b, w1, w2, l2g, l2b, *, tm, alpha):
    b, s, d = x.shape
    tile = pl.BlockSpec((1, tm, d), lambda i, j: (i, j, 0))
    const = lambda shape: pl.BlockSpec(shape, lambda i, j: (0,) * len(shape),
                                       pipeline_mode=pl.Buffered(1))
    return pl.pallas_call(
        functools.partial(_out_ffn_kernel, alpha=alpha),
        grid=(b, s // tm),
        in_specs=[tile, tile, pl.BlockSpec((1, 6, d), lambda i, j: (i, 0, 0)),
                  const(wout.shape), const((1, d)), const((1, d)),
                  const(w1.shape), const(w2.shape), const((1, d)), const((1, d))],
        out_specs=tile,
        out_shape=jax.ShapeDtypeStruct((b, s, d), F32),
        compiler_params=_params(2),
        name="out_ffn",
    )(y, x, mod6, wout, l1g, l1b, w1, w2, l2g, l2b)


def _sw_head_order():
    return [kv * SW_GROUP + j for j in range(SW_GROUP) for kv in range(SW_KV_HEADS)]


def _rope_tables(seq):
    pos = jnp.arange(seq)
    row_pos, col_pos = pos // GRID_W, pos % GRID_W
    inv_freq = ROPE_BASE ** (-jnp.arange(ROPE_FREQS, dtype=F32) / ROPE_FREQS)
    ang_r = row_pos.astype(F32)[:, None] * inv_freq
    ang_c = col_pos.astype(F32)[:, None] * inv_freq
    cos_head = jnp.concatenate([jnp.cos(ang_r), jnp.cos(ang_r), jnp.cos(ang_c), jnp.cos(ang_c)], axis=-1)
    sin_head = jnp.concatenate([-jnp.sin(ang_r), jnp.sin(ang_r), -jnp.sin(ang_c), jnp.sin(ang_c)], axis=-1)
    return jnp.tile(cos_head, (1, LANES // HEAD_DIM)), jnp.tile(sin_head, (1, LANES // HEAD_DIM))


def _na_bias_table(rpb):
    cq = np.arange(GRID_W)
    cs = np.clip(cq - NA_COLS // 2, 0, GRID_W - NA_COLS)
    col_ok = (cq[None, :] >= cs[:, None]) & (cq[None, :] < cs[:, None] + NA_COLS)
    ext = GRID_W - NA_COLS
    rpb_ext = jnp.concatenate([jnp.repeat(rpb[..., :1], ext, axis=-1), rpb,
                               jnp.repeat(rpb[..., -1:], ext, axis=-1)], axis=-1)
    rel = jnp.stack([rpb_ext[..., GRID_W - 1 - q:2 * GRID_W - 1 - q] for q in range(GRID_W)], axis=2)
    bias_col = jnp.where(col_ok[None, None], rel * LOG2E, NEG_INF)
    return bias_col.transpose(1, 3, 0, 2).reshape(2 * NA_ROWS - 1, GRID_W, NA_HEADS * GRID_W)


def _sw_mask_table():
    kj = np.arange(SW_BLOCK)[:, None]
    qi = np.arange(SW_BLOCK)[None, :]
    prev = np.where(kj >= qi, 0.0, NEG_INF)
    nxt = np.where(kj <= qi, 0.0, NEG_INF)
    none = np.full((SW_BLOCK, SW_BLOCK), NEG_INF)
    return jnp.asarray(np.stack([np.tile(m, (1, SW_HEADS)) for m in (prev, nxt, none)]), F32)


def kernel(x, c, ctx, c_ctx, w_mod, b_mod, w_in, a_ln_g, a_ln_b, a_ws, a_bs, na_rpb, sw_sink, w_out,
           ln1_g, ln1_b, w1, w2, ln2_g, ln2_b):
    depth = w_mod.shape[0]
    b, s, d = x.shape
    lc = ctx.shape[1]
    alpha = (2 * depth) ** 0.25
    tq = 512
    tm = 1024
    assert s % tm == 0 and tm % tq == 0 and (b * lc) % tm == 0 and lc % CHUNK == 0
    ctx_nb = math.gcd(b, max(1, tm // lc))

    rows = -(-(b + 1) // MOD_ROWS_PAD) * MOD_ROWS_PAD
    cc = jnp.concatenate([c, c_ctx[None], jnp.zeros((rows - b - 1, d), F32)], axis=0)
    mod_all = _modulation(cc, w_mod, b_mod)

    q0 = 2 * A_WIDTH + 3 * NA_WIDTH
    nav0 = 2 * A_WIDTH + 2 * NA_WIDTH
    k0 = q0 + SW_Q_WIDTH
    heads = _sw_head_order()
    cos, sin = _rope_tables(s)
    sw_mask = _sw_mask_table()

    xc = ctx
    for layer in range(depth):
        mod6 = mod_all[layer, :b].reshape(b, 6, d)
        modc6 = jnp.broadcast_to(mod_all[layer, b].reshape(1, 6, d), (b, 6, d))
        wl = w_in[layer]
        w_main = jnp.concatenate(
            [wl[:, :nav0]] + [wl[:, q0 + h * HEAD_DIM:q0 + (h + 1) * HEAD_DIM] for h in heads]
            + [wl[:, k0:k0 + SW_KV_WIDTH]], axis=1).astype(BF16)
        w_vt = jnp.concatenate([wl[:, nav0:q0], wl[:, k0 + SW_KV_WIDTH:]], axis=1).T.astype(BF16)
        wo = w_out[layer]
        y0 = A_WIDTH + NA_WIDTH
        w_out_l = jnp.concatenate(
            [wo[:y0]] + [wo[y0 + h * HEAD_DIM:y0 + (h + 1) * HEAD_DIM] for h in heads], axis=0).astype(BF16)
        w1_l = w1[layer].astype(BF16)
        w2_l = w2[layer].astype(BF16)
        lng = a_ln_g[layer].reshape(1, A_WIDTH)
        lnb = a_ln_b[layer].reshape(1, A_WIDTH)
        ws_all = a_ws[layer].reshape(A_GROUPS * CHUNK, CHUNK).astype(BF16)
        bsm = jnp.repeat(a_bs[layer].T, HEAD_DIM, axis=1)
        bias_tab = _na_bias_table(na_rpb[layer])
        sink8 = sw_sink[layer] * LOG2E
        sink_blk = jnp.repeat(sink8, SW_BLOCK)[None, :]
        sink_ctx = jnp.repeat(sink8, lc)[None, :]
        l1g, l1b = ln1_g[layer].reshape(1, d), ln1_b[layer].reshape(1, d)
        l2g, l2b = ln2_g[layer].reshape(1, d), ln2_b[layer].reshape(1, d)

        u, v, naq, nak, navt, swq, swk, swvt = _in_projection(
            x, mod6, w_main, w_vt, lng, lnb, cos, sin, rope=True, tm=tm)
        last = layer == depth - 1
        ctx_proj = _in_projection(xc, modc6, w_main, w_vt, lng, lnb, cos, sin, rope=False, tm=lc,
                                  nb=ctx_nb, kv_only=last)
        if last:
            cnak, cnavt, cswk, cswvt = ctx_proj
        else:
            cu, cv, cnaq, cnak, cnavt, cswq, cswk, cswvt = ctx_proj

        y = _mixer(u, v, naq, nak, navt, swq, swk, swvt, cnak, cnavt, cswk, cswvt,
                   ws_all, bsm, bias_tab, sw_mask, sink_blk, tq=tq)
        x = _out_ffn(y, x, mod6, w_out_l, l1g, l1b, w1_l, w2_l, l2g, l2b, tm=tm, alpha=alpha)

        if not last:
            yc = _ctx_mixer(cu, cv, cnaq, cnak, cnavt, cswq, cswk, cswvt, ws_all, bsm, sink_ctx)
            groups = b * lc // tm
            xc = _out_ffn(yc.reshape(groups, tm, d), xc.reshape(groups, tm, d), modc6[:groups], w_out_l, l1g, l1b,
                          w1_l, w2_l, l2g, l2b, tm=tm, alpha=alpha).reshape(b, lc, d)
    return x
```
